```python
import math
import jax, jax.numpy as jnp
from jax import lax
import numpy as np

D_MODEL = 1024
BATCH = 32
SEQ = 2048
DEPTH = 2

MLA_HEADS = 8
MLA_NOPE = 64
MLA_ROPE = 32
MLA_V = 64
Q_LORA = 256
KV_LORA = 256
DIFF_HEADS = 4
DIFF_D = 64
D_FF = 4 * D_MODEL

MLA_WIDTH = MLA_HEADS * MLA_V
DIFF_WIDTH = DIFF_HEADS * 2 * DIFF_D
D_MIX = MLA_WIDTH + DIFF_WIDTH

OFF_CQ = 0
OFF_CKV = OFF_CQ + Q_LORA
OFF_KR = OFF_CKV + KV_LORA
OFF_DQ = OFF_KR + MLA_ROPE
OFF_DK = OFF_DQ + DIFF_HEADS * 2 * DIFF_D
OFF_DV = OFF_DK + DIFF_HEADS * 2 * DIFF_D
P_IN = OFF_DV + DIFF_HEADS * 2 * DIFF_D

Q_BLOCK = 128
ROPE_THETA = 10000.0
EPS = 1e-6
SUBLN_EPS = 1e-5

kernel_name = "hybrid_mla_diffattn_alibi_encoder"


def _rmsnorm(x, g, eps=EPS):
    xf = x.astype(jnp.float32)
    y = xf * lax.rsqrt(jnp.mean(xf * xf, axis=-1, keepdims=True) + eps)
    return (y * g.astype(jnp.float32)).astype(x.dtype)


def _rope(x, cos, sin):
    half = x.shape[-1] // 2
    xf = x.astype(jnp.float32)
    x1, x2 = xf[..., :half], xf[..., half:]
    return jnp.concatenate([x1 * cos - x2 * sin, x1 * sin + x2 * cos], axis=-1).astype(x.dtype)


def _alibi_slopes(n_heads):
    return jnp.exp2(-8.0 * (jnp.arange(n_heads, dtype=jnp.float32) + 1.0) / n_heads)


def _to_blocks(t):
    b, s = t.shape[:2]
    t = t.reshape((b, s // Q_BLOCK, Q_BLOCK) + t.shape[2:])
    return jnp.moveaxis(t, 1, 0)


def _from_blocks(t):
    t = jnp.moveaxis(t, 0, 1)
    return t.reshape((t.shape[0], t.shape[1] * t.shape[2]) + t.shape[3:])


def _mla_attention(q, k, v):
    scale = (MLA_NOPE + MLA_ROPE) ** -0.5

    def one_block(qb):
        s = jnp.einsum('bqhd,bkhd->bhqk', qb, k, preferred_element_type=jnp.float32) * scale
        p = jax.nn.softmax(s, axis=-1).astype(v.dtype)
        return jnp.einsum('bhqk,bkhd->bqhd', p, v)

    return _from_blocks(lax.map(one_block, _to_blocks(q)))


def _diff_attention(q, k, v, lam, slopes):
    s_len = k.shape[1]
    scale = DIFF_D ** -0.5
    kpos = jnp.arange(s_len, dtype=jnp.float32)
    starts = jnp.arange(s_len // Q_BLOCK, dtype=jnp.int32) * Q_BLOCK

    def one_block(args):
        qb, start = args
        qpos = (start + jnp.arange(Q_BLOCK, dtype=jnp.int32)).astype(jnp.float32)
        bias = -slopes[:, None, None] * jnp.abs(qpos[:, None] - kpos[None, :])
        s = jnp.einsum('bqhmd,bkhmd->bmhqk', qb, k,
                       preferred_element_type=jnp.float32) * scale + bias
        p = jax.nn.softmax(s, axis=-1)
        w = (p[:, 0] - lam * p[:, 1]).astype(v.dtype)
        return jnp.einsum('bhqk,bkhd->bqhd', w, v)

    return _from_blocks(lax.map(one_block, (_to_blocks(q), starts)))


def setup_inputs(seed: int = 0) -> dict:
    key = jax.random.key(seed)
    ks = jax.random.split(key, 18)
    f32 = jnp.float32
    L = DEPTH

    def w(k, shape, fan_in):
        return jax.random.normal(k, shape, f32) * (fan_in ** -0.5)

    def gain(k, shape):
        return 1.0 + 0.01 * jax.random.normal(k, shape, f32)

    return {
        "x": jax.random.normal(ks[0], (BATCH, SEQ, D_MODEL), f32),
        "w_in": w(ks[1], (L, D_MODEL, P_IN), D_MODEL),
        "g_mix": gain(ks[2], (L, D_MODEL)),
        "g_q": gain(ks[3], (L, Q_LORA)),
        "g_kv": gain(ks[4], (L, KV_LORA)),
        "w_uq": w(ks[5], (L, Q_LORA, MLA_HEADS * (MLA_NOPE + MLA_ROPE)), Q_LORA),
        "w_ukv": w(ks[6], (L, KV_LORA, MLA_HEADS * (MLA_NOPE + MLA_V)), KV_LORA),
        "lam_q1": 0.1 * jax.random.normal(ks[7], (L, DIFF_D), f32),
        "lam_k1": 0.1 * jax.random.normal(ks[8], (L, DIFF_D), f32),
        "lam_q2": 0.1 * jax.random.normal(ks[9], (L, DIFF_D), f32),
        "lam_k2": 0.1 * jax.random.normal(ks[10], (L, DIFF_D), f32),
        "g_sub": gain(ks[11], (L, 2 * DIFF_D)),
        "w_out": w(ks[12], (L, D_MIX, D_MODEL), D_MIX),
        "g_mlp": gain(ks[13], (L, D_MODEL)),
        "w_up": w(ks[14], (L, D_MODEL, D_FF), D_MODEL),
        "w_down": w(ks[15], (L, D_FF, D_MODEL), D_FF),
        "g_final": gain(ks[16], (D_MODEL,)),
    }


def reference(x, w_in, g_mix, g_q, g_kv, w_uq, w_ukv, lam_q1, lam_k1, lam_q2, lam_k2,
              g_sub, w_out, g_mlp, w_up, w_down, g_final):
    b, s, _ = x.shape
    pos = jnp.arange(s, dtype=jnp.float32)
    inv_freq = 1.0 / (ROPE_THETA ** (jnp.arange(0, MLA_ROPE, 2, dtype=jnp.float32) / MLA_ROPE))
    ang = pos[:, None] * inv_freq[None, :]
    cos, sin = jnp.cos(ang), jnp.sin(ang)
    slopes = _alibi_slopes(DIFF_HEADS)

    h = x
    for l in range(DEPTH):
        lam_init = 0.8 - 0.6 * math.exp(-0.3 * l)
        u = _rmsnorm(h, g_mix[l])
        z = u @ w_in[l]

        c_q = _rmsnorm(z[..., OFF_CQ:OFF_CKV], g_q[l])
        c_kv = _rmsnorm(z[..., OFF_CKV:OFF_KR], g_kv[l])
        k_rope = _rope(z[..., OFF_KR:OFF_DQ], cos, sin)
        q = (c_q @ w_uq[l]).reshape(b, s, MLA_HEADS, MLA_NOPE + MLA_ROPE)
        q = jnp.concatenate(
            [q[..., :MLA_NOPE], _rope(q[..., MLA_NOPE:], cos[:, None], sin[:, None])], axis=-1)
        kv = (c_kv @ w_ukv[l]).reshape(b, s, MLA_HEADS, MLA_NOPE + MLA_V)
        k = jnp.concatenate(
            [kv[..., :MLA_NOPE],
             jnp.broadcast_to(k_rope[:, :, None, :], (b, s, MLA_HEADS, MLA_ROPE))], axis=-1)
        o_mla = _mla_attention(q, k, kv[..., MLA_NOPE:]).reshape(b, s, MLA_WIDTH)

        dq = z[..., OFF_DQ:OFF_DK].reshape(b, s, DIFF_HEADS, 2, DIFF_D)
        dk = z[..., OFF_DK:OFF_DV].reshape(b, s, DIFF_HEADS, 2, DIFF_D)
        dv = z[..., OFF_DV:P_IN].reshape(b, s, DIFF_HEADS, 2 * DIFF_D)
        lam = (jnp.exp(jnp.sum(lam_q1[l].astype(jnp.float32) * lam_k1[l].astype(jnp.float32)))
               - jnp.exp(jnp.sum(lam_q2[l].astype(jnp.float32) * lam_k2[l].astype(jnp.float32)))
               + lam_init)
        o = _diff_attention(dq, dk, dv, lam, slopes)
        o = _rmsnorm(o, g_sub[l], SUBLN_EPS) * (1.0 - lam_init)
        o_diff = o.reshape(b, s, DIFF_WIDTH)

        h = h + jnp.concatenate([o_mla, o_diff], axis=-1) @ w_out[l]

        u = _rmsnorm(h, g_mlp[l])
        h = h + jnp.square(jax.nn.relu(u @ w_up[l])) @ w_down[l]

    return _rmsnorm(h, g_final)
```

```python
import functools
import math

import numpy as np
import jax
import jax.numpy as jnp
from jax import lax
from jax.experimental import pallas as pl
from jax.experimental.pallas import tpu as pltpu

D_MODEL = 1024
MLA_HEADS = 8
MLA_NOPE = 64
MLA_ROPE = 32
MLA_V = 64
Q_LORA = 256
KV_LORA = 256
DIFF_HEADS = 4
DIFF_D = 64
D_FF = 4 * D_MODEL
ROPE_THETA = 10000.0
EPS = 1e-6
SUBLN_EPS = 1e-5

LANES = 128
LOG2E = math.log2(math.e)

C_CQ = 0
C_CKV = C_CQ + Q_LORA
C_KR = C_CKV + KV_LORA
C_DQ = C_KR + LANES
C_DK = C_DQ + DIFF_HEADS * LANES
C_DV = C_DK + DIFF_HEADS * LANES
P_PACK = C_DV + DIFF_HEADS * LANES

TM_PROJ = 512
TM_MLP = 512
TQ_MLA = 256
TQ_DIFF = 256
FF_CHUNK = 1024
VMEM_LIMIT = 56 * 1024 * 1024

F32 = jnp.float32
BF16 = jnp.bfloat16


def _rms(x, g, eps):
    return x * lax.rsqrt(jnp.mean(x * x, axis=-1, keepdims=True) + eps) * g


def _dot(a, b):
    return jnp.dot(a, b, preferred_element_type=F32)


def _dot_nt(a, b):
    return lax.dot_general(a, b, (((1,), (1,)), ((), ())), preferred_element_type=F32)


def _proj_kernel(x_ref, gmix_ref, w1_ref, gq_ref, gkv_ref, wq_ref, wkv_ref,
                 cos_ref, sa_ref, sb_ref, vone_ref,
                 qm_ref, km_ref, vm_ref, dq_ref, dk_ref, dv_ref):
    x = x_ref[...]
    u = _rms(x, gmix_ref[...], EPS).astype(BF16)
    z = _dot(u, w1_ref[...])
    cq = _rms(z[:, C_CQ:C_CKV], gq_ref[...], EPS).astype(BF16)
    ckv = _rms(z[:, C_CKV:C_KR], gkv_ref[...], EPS).astype(BF16)
    q = _dot(cq, wq_ref[...])
    kv = _dot(ckv, wkv_ref[...])
    cos = cos_ref[...]
    sa = sa_ref[...]
    sb = sb_ref[...]

    def rope(t):
        return t * cos + pltpu.roll(t, LANES - 16, 1) * sa + pltpu.roll(t, 16, 1) * sb

    k_rope = rope(z[:, C_KR:C_DQ])
    q_scale = (MLA_NOPE + MLA_ROPE) ** -0.5 * LOG2E
    nk = MLA_HEADS * LANES
    v_all = kv[:, nk:] + vone_ref[...]
    for h in range(MLA_HEADS):
        sl = slice(h * LANES, (h + 1) * LANES)
        qm_ref[0, h] = (rope(q[:, sl]) * q_scale).astype(BF16)
        km_ref[0, h] = (kv[:, sl] + k_rope).astype(BF16)
        vm_ref[0, h] = v_all[:, sl].astype(BF16)
    d_scale = DIFF_D ** -0.5 * LOG2E
    for h in range(DIFF_HEADS):
        dq_ref[0, h] = (z[:, C_DQ + h * LANES:C_DQ + (h + 1) * LANES] * d_scale).astype(BF16)
        dk_ref[0, h] = z[:, C_DK + h * LANES:C_DK + (h + 1) * LANES].astype(BF16)
        dv_ref[0, h] = z[:, C_DV + h * LANES:C_DV + (h + 1) * LANES].astype(BF16)


def _const_spec(shape):
    nd = len(shape)
    return pl.BlockSpec(shape, lambda *_: (0,) * nd, pipeline_mode=pl.Buffered(1))


def _proj_call(xf, gmix, w1, gq, gkv, wq, wkv, cos_t, sa_t, sb_t, vone, batch, seq):
    tm = TM_PROJ
    nsb = seq // tm
    tokens = batch * seq
    head_spec = lambda nh: pl.BlockSpec((1, nh, tm, LANES), lambda i: (i // nsb, 0, i % nsb, 0))
    tab_spec = pl.BlockSpec((tm, LANES), lambda i: (i % nsb, 0))
    mshape = jax.ShapeDtypeStruct((batch, MLA_HEADS, seq, LANES), BF16)
    dshape = jax.ShapeDtypeStruct((batch, DIFF_HEADS, seq, LANES), BF16)
    return pl.pallas_call(
        _proj_kernel,
        grid=(tokens // tm,),
        in_specs=[
            pl.BlockSpec((tm, D_MODEL), lambda i: (i, 0)),
            _const_spec(gmix.shape), _const_spec(w1.shape), _const_spec(gq.shape),
            _const_spec(gkv.shape), _const_spec(wq.shape), _const_spec(wkv.shape),
            tab_spec, tab_spec, tab_spec, _const_spec(vone.shape),
        ],
        out_specs=[head_spec(MLA_HEADS)] * 3 + [head_spec(DIFF_HEADS)] * 3,
        out_shape=[mshape] * 3 + [dshape] * 3,
        compiler_params=pltpu.CompilerParams(
            dimension_semantics=("arbitrary",), vmem_limit_bytes=VMEM_LIMIT),
        name="proj",
    )(xf, gmix, w1, gq, gkv, wq, wkv, cos_t, sa_t, sb_t, vone)


def _mla_kernel(q_ref, k_ref, v_ref, o_ref, *, seq):
    tq = TQ_MLA
    lane = lax.broadcasted_iota(jnp.int32, (tq, LANES), 1)

    def body(i, carry):
        r0 = pl.multiple_of(i * tq, tq)
        outs = []
        for hh in range(2):
            q = q_ref[0, hh, pl.ds(r0, tq), :]
            s = _dot_nt(q, k_ref[0, hh])
            m = jnp.max(s, axis=-1, keepdims=True)
            p = jnp.exp2(s - m).astype(BF16)
            oa = _dot(p, v_ref[0, hh])
            lcol = MLA_V if hh == 0 else 0
            outs.append(oa * (1.0 / oa[:, lcol:lcol + 1]))
        o_ref[0, pl.ds(r0, tq), :] = jnp.where(lane < MLA_V, outs[0], outs[1]).astype(o_ref.dtype)
        return carry

    lax.fori_loop(0, seq // tq, body, 0)


def _mla_call(qm, km, vm):
    batch, _, seq, _ = qm.shape
    in_spec = pl.BlockSpec((1, 2, seq, LANES), lambda b, g: (b, g, 0, 0))
    return pl.pallas_call(
        functools.partial(_mla_kernel, seq=seq),
        grid=(batch, MLA_HEADS // 2),
        in_specs=[in_spec, in_spec, in_spec],
        out_specs=pl.BlockSpec((1, seq, LANES), lambda b, g: (b, 0, g)),
        out_shape=jax.ShapeDtypeStruct((batch, seq, MLA_HEADS * MLA_V), BF16),
        compiler_params=pltpu.CompilerParams(
            dimension_semantics=("arbitrary", "arbitrary"), vmem_limit_bytes=VMEM_LIMIT),
        name="mla_attn",
    )(qm, km, vm)


def _diff_kernel(lam_ref, gsub_ref, q_ref, k_ref, v_ref, o_ref, bias_ref, k12_ref, vaug_ref,
                 *, seq, lam_init):
    tq = TQ_DIFF
    b = pl.program_id(0)
    h = pl.program_id(1)
    wide = 2 * seq - tq

    @pl.when(b == 0)
    def _():
        r = lax.broadcasted_iota(jnp.int32, (tq, wide), 0)
        c = lax.broadcasted_iota(jnp.int32, (tq, wide), 1)
        dist = jnp.abs(r - (c - (seq - tq))).astype(F32)
        hv = jnp.full((1, 1), h, jnp.int32)
        slope = jnp.zeros((1, 1), F32)
        for hh in range(DIFF_HEADS):
            slope = jnp.where(hv == hh, 2.0 ** (-8.0 * (hh + 1.0) / DIFF_HEADS), slope)
        bias_ref[h] = dist * (-LOG2E * slope)

    lp = lam_ref[...]
    lam = (jnp.exp(jnp.sum(lp[0:1] * lp[1:2], axis=-1, keepdims=True))
           - jnp.exp(jnp.sum(lp[2:3] * lp[3:4], axis=-1, keepdims=True)) + lam_init)

    k = k_ref[0, 0]
    lane_k = lax.broadcasted_iota(jnp.int32, k.shape, 1)
    zero = jnp.zeros_like(k)
    k12_ref[0] = jnp.where(lane_k < DIFF_D, k, zero)
    k12_ref[1] = jnp.where(lane_k >= DIFF_D, k, zero)
    vaug_ref[:, :LANES] = v_ref[0, 0]
    vaug_ref[:, LANES:] = (lane_k == 0).astype(BF16)
    g = gsub_ref[...] * (1.0 - lam_init)

    def body(i, carry):
        r0 = pl.multiple_of(i * tq, tq)
        q = q_ref[0, 0, pl.ds(r0, tq), :]
        c0 = pl.multiple_of((seq - tq) - i * tq, LANES)
        bias = bias_ref[h, :, pl.ds(c0, seq)]
        outs = []
        for mi in range(2):
            s = _dot_nt(q, k12_ref[mi]) + bias
            m = jnp.max(s, axis=-1, keepdims=True)
            p = jnp.exp2(s - m).astype(BF16)
            oa = _dot(p, vaug_ref[...])
            outs.append(oa[:, :LANES] * (1.0 / oa[:, LANES:LANES + 1]))
        o = outs[0] - lam * outs[1]
        o_ref[0, pl.ds(r0, tq), :] = _rms(o, g, SUBLN_EPS).astype(o_ref.dtype)
        return carry

    lax.fori_loop(0, seq // tq, body, 0)


def _diff_call(lam_p, gsub, dq, dk, dv, lam_init):
    batch, _, seq, _ = dq.shape
    in_spec = pl.BlockSpec((1, 1, seq, LANES), lambda b, h: (b, h, 0, 0))
    return pl.pallas_call(
        functools.partial(_diff_kernel, seq=seq, lam_init=lam_init),
        grid=(batch, DIFF_HEADS),
        in_specs=[_const_spec(lam_p.shape), _const_spec(gsub.shape), in_spec, in_spec, in_spec],
        out_specs=pl.BlockSpec((1, seq, LANES), lambda b, h: (b, 0, h)),
        out_shape=jax.ShapeDtypeStruct((batch, seq, DIFF_HEADS * LANES), BF16),
        scratch_shapes=[pltpu.VMEM((DIFF_HEADS, TQ_DIFF, 2 * seq - TQ_DIFF), F32),
                        pltpu.VMEM((2, seq, LANES), BF16),
                        pltpu.VMEM((seq, 2 * LANES), BF16)],
        compiler_params=pltpu.CompilerParams(
            dimension_semantics=("arbitrary", "arbitrary"), vmem_limit_bytes=VMEM_LIMIT),
        name="diff_attn",
    )(lam_p, gsub, dq, dk, dv)


def _mlp_kernel(om_ref, od_ref, h_ref, wo1_ref, wo2_ref, gmlp_ref, wup_ref, wdn_ref, gfin_ref,
                out_ref, *, final):
    a = h_ref[...] + _dot(om_ref[...], wo1_ref[...]) + _dot(od_ref[...], wo2_ref[...])
    u = _rms(a, gmlp_ref[...], EPS).astype(BF16)
    out_ref[...] = a
    for c in range(D_FF // FF_CHUNK):
        f = jnp.maximum(_dot(u, wup_ref[:, c * FF_CHUNK:(c + 1) * FF_CHUNK]), 0.0)
        out_ref[...] += _dot((f * f).astype(BF16), wdn_ref[c * FF_CHUNK:(c + 1) * FF_CHUNK, :])
    if final:
        out_ref[...] = _rms(out_ref[...], gfin_ref[...], EPS)


def _mlp_call(om, od, hf, wo1, wo2, gmlp, wup, wdn, gfin, final):
    tm = TM_MLP
    tokens = hf.shape[0]
    row = lambda w: pl.BlockSpec((tm, w), lambda i: (i, 0))
    return pl.pallas_call(
        functools.partial(_mlp_kernel, final=final),
        grid=(tokens // tm,),
        in_specs=[row(om.shape[1]), row(od.shape[1]), row(D_MODEL),
                  _const_spec(wo1.shape), _const_spec(wo2.shape), _const_spec(gmlp.shape),
                  _const_spec(wup.shape), _const_spec(wdn.shape), _const_spec(gfin.shape)],
        out_specs=row(D_MODEL),
        out_shape=jax.ShapeDtypeStruct(hf.shape, F32),
        compiler_params=pltpu.CompilerParams(
            dimension_semantics=("arbitrary",), vmem_limit_bytes=VMEM_LIMIT),
        name="outproj_mlp",
    )(om, od, hf, wo1, wo2, gmlp, wup, wdn, gfin)


def _pack_w_in(w):
    zeros = lambda n: jnp.zeros((D_MODEL, n), w.dtype)
    off_kr = Q_LORA + KV_LORA
    off_dq = off_kr + MLA_ROPE
    kr = jnp.concatenate([zeros(MLA_NOPE), w[:, off_kr:off_dq], zeros(LANES - MLA_NOPE - MLA_ROPE)], 1)
    return jnp.concatenate([w[:, :off_kr], kr, w[:, off_dq:]], axis=1).astype(BF16)


def _pack_w_uq(w):
    hd = MLA_NOPE + MLA_ROPE
    w = w.reshape(Q_LORA, MLA_HEADS, hd)
    w = jnp.pad(w, ((0, 0), (0, 0), (0, LANES - hd)))
    return w.reshape(Q_LORA, MLA_HEADS * LANES).astype(BF16)


def _pack_w_ukv(w):
    w = w.reshape(KV_LORA, MLA_HEADS, MLA_NOPE + MLA_V)
    wk = jnp.pad(w[:, :, :MLA_NOPE], ((0, 0), (0, 0), (0, LANES - MLA_NOPE)))
    wv = w[:, :, MLA_NOPE:]
    pad = jnp.zeros_like(wv)
    wv_even = jnp.concatenate([wv, pad], axis=-1)
    wv_odd = jnp.concatenate([pad, wv], axis=-1)
    odd = (jnp.arange(MLA_HEADS) % 2 == 1)[None, :, None]
    wv = jnp.where(odd, wv_odd, wv_even)
    return jnp.concatenate([wk.reshape(KV_LORA, -1), wv.reshape(KV_LORA, -1)], axis=1).astype(BF16)


def _rope_tables(seq):
    pos = np.arange(seq, dtype=np.float32)
    inv_freq = (1.0 / (ROPE_THETA ** (np.arange(0, MLA_ROPE, 2, dtype=np.float32) / MLA_ROPE))).astype(np.float32)
    ang = jnp.asarray(pos[:, None] * inv_freq[None, :])
    cos, sin = jnp.cos(ang), jnp.sin(ang)
    half = MLA_ROPE // 2
    ones = jnp.ones((seq, MLA_NOPE), F32)
    z = lambda n: jnp.zeros((seq, n), F32)
    tail = LANES - MLA_NOPE - MLA_ROPE
    cos_t = jnp.concatenate([ones, cos, cos, z(tail)], axis=1)
    sa_t = jnp.concatenate([z(MLA_NOPE), -sin, z(half), z(tail)], axis=1)
    sb_t = jnp.concatenate([z(MLA_NOPE), z(half), sin, z(tail)], axis=1)
    return cos_t, sa_t, sb_t


def _v_ones():
    v = np.zeros((1, MLA_HEADS * LANES), np.float32)
    for h in range(MLA_HEADS):
        v[0, h * LANES + (MLA_V if h % 2 == 0 else 0)] = 1.0
    return jnp.asarray(v)


def kernel(x, w_in, g_mix, g_q, g_kv, w_uq, w_ukv, lam_q1, lam_k1, lam_q2, lam_k2,
           g_sub, w_out, g_mlp, w_up, w_down, g_final):
    batch, seq, _ = x.shape
    depth = w_in.shape[0]
    cos_t, sa_t, sb_t = _rope_tables(seq)
    vone = _v_ones()
    hf = x.reshape(batch * seq, D_MODEL)
    for l in range(depth):
        lam_init = 0.8 - 0.6 * math.exp(-0.3 * l)
        qm, km, vm, dq, dk, dv = _proj_call(
            hf, g_mix[l][None], _pack_w_in(w_in[l]), g_q[l][None], g_kv[l][None],
            _pack_w_uq(w_uq[l]), _pack_w_ukv(w_ukv[l]), cos_t, sa_t, sb_t, vone, batch, seq)
        o_mla = _mla_call(qm, km, vm)
        lam_p = jnp.stack([lam_q1[l], lam_k1[l], lam_q2[l], lam_k2[l]]).astype(F32)
        o_diff = _diff_call(lam_p, g_sub[l][None], dq, dk, dv, lam_init)
        wo = w_out[l].astype(BF16)
        n_mla = MLA_HEADS * MLA_V
        hf = _mlp_call(o_mla.reshape(batch * seq, -1), o_diff.reshape(batch * seq, -1), hf,
                       wo[:n_mla], wo[n_mla:], g_mlp[l][None], w_up[l].astype(BF16),
                       w_down[l].astype(BF16), g_final[None], final=(l == depth - 1))
    return hf.reshape(batch, seq, D_MODEL)
```

```python
import functools
import math

import numpy as np
import jax
import jax.numpy as jnp
from jax import lax
from jax.experimental import pallas as pl
from jax.experimental.pallas import tpu as pltpu

D_MODEL = 1024
MLA_HEADS = 8
MLA_NOPE = 64
MLA_ROPE = 32
MLA_V = 64
Q_LORA = 256
KV_LORA = 256
DIFF_HEADS = 4
DIFF_D = 64
D_FF = 4 * D_MODEL
ROPE_THETA = 10000.0
EPS = 1e-6
SUBLN_EPS = 1e-5

LANES = 128
LOG2E = math.log2(math.e)

C_CQ = 0
C_CKV = C_CQ + Q_LORA
C_KR = C_CKV + KV_LORA
C_DQ = C_KR + LANES
C_DK = C_DQ + DIFF_HEADS * LANES
C_DV = C_DK + DIFF_HEADS * LANES
P_PACK = C_DV + DIFF_HEADS * LANES

TM_PROJ = 512
TM_MLP = 512
TQ_MLA = 256
TQ_DIFF = 256
FF_CHUNK = 1024
KEY_TILE = 256
VMEM_LIMIT = 56 * 1024 * 1024

F32 = jnp.float32
BF16 = jnp.bfloat16


def _rms(x, g, eps):
    return x * lax.rsqrt(jnp.mean(x * x, axis=-1, keepdims=True) + eps) * g


def _dot(a, b):
    return jnp.dot(a, b, preferred_element_type=F32)


def _dot_nt(a, b):
    return lax.dot_general(a, b, (((1,), (1,)), ((), ())), preferred_element_type=F32)


def _proj_kernel(x_ref, gmix_ref, w1_ref, gq_ref, gkv_ref, wq_ref, wkv_ref,
                 cos_ref, sa_ref, sb_ref, vone_ref,
                 qm_ref, km_ref, vm_ref, dq_ref, dk_ref, dv_ref):
    x = x_ref[...]
    u = _rms(x, gmix_ref[...], EPS).astype(BF16)
    z = _dot(u, w1_ref[...])
    cq = _rms(z[:, C_CQ:C_CKV], gq_ref[...], EPS).astype(BF16)
    ckv = _rms(z[:, C_CKV:C_KR], gkv_ref[...], EPS).astype(BF16)
    q = _dot(cq, wq_ref[...])
    kv = _dot(ckv, wkv_ref[...])
    cos = cos_ref[...]
    sa = sa_ref[...]
    sb = sb_ref[...]

    def rope(t):
        return t * cos + pltpu.roll(t, LANES - 16, 1) * sa + pltpu.roll(t, 16, 1) * sb

    k_rope = rope(z[:, C_KR:C_DQ])
    q_scale = (MLA_NOPE + MLA_ROPE) ** -0.5 * LOG2E
    nk = MLA_HEADS * LANES
    v_all = kv[:, nk:] + vone_ref[...]
    for h in range(MLA_HEADS):
        sl = slice(h * LANES, (h + 1) * LANES)
        qm_ref[0, h] = (rope(q[:, sl]) * q_scale).astype(BF16)
        km_ref[0, h] = (kv[:, sl] + k_rope).astype(BF16)
        vm_ref[0, h] = v_all[:, sl].astype(BF16)
    d_scale = DIFF_D ** -0.5 * LOG2E
    for h in range(DIFF_HEADS):
        dq_ref[0, h] = (z[:, C_DQ + h * LANES:C_DQ + (h + 1) * LANES] * d_scale).astype(BF16)
        dk_ref[0, h] = z[:, C_DK + h * LANES:C_DK + (h + 1) * LANES].astype(BF16)
        dv_ref[0, h] = z[:, C_DV + h * LANES:C_DV + (h + 1) * LANES].astype(BF16)


def _const_spec(shape):
    nd = len(shape)
    return pl.BlockSpec(shape, lambda *_: (0,) * nd, pipeline_mode=pl.Buffered(1))


def _proj_call(xf, gmix, w1, gq, gkv, wq, wkv, cos_t, sa_t, sb_t, vone, batch, seq):
    tm = TM_PROJ
    nsb = seq // tm
    tokens = batch * seq
    head_spec = lambda nh: pl.BlockSpec((1, nh, tm, LANES), lambda i: (i // nsb, 0, i % nsb, 0))
    tab_spec = pl.BlockSpec((tm, LANES), lambda i: (i % nsb, 0))
    mshape = jax.ShapeDtypeStruct((batch, MLA_HEADS, seq, LANES), BF16)
    dshape = jax.ShapeDtypeStruct((batch, DIFF_HEADS, seq, LANES), BF16)
    return pl.pallas_call(
        _proj_kernel,
        grid=(tokens // tm,),
        in_specs=[
            pl.BlockSpec((tm, D_MODEL), lambda i: (i, 0)),
            _const_spec(gmix.shape), _const_spec(w1.shape), _const_spec(gq.shape),
            _const_spec(gkv.shape), _const_spec(wq.shape), _const_spec(wkv.shape),
            tab_spec, tab_spec, tab_spec, _const_spec(vone.shape),
        ],
        out_specs=[head_spec(MLA_HEADS)] * 3 + [head_spec(DIFF_HEADS)] * 3,
        out_shape=[mshape] * 3 + [dshape] * 3,
        compiler_params=pltpu.CompilerParams(
            dimension_semantics=("arbitrary",), vmem_limit_bytes=VMEM_LIMIT),
        name="proj",
    )(xf, gmix, w1, gq, gkv, wq, wkv, cos_t, sa_t, sb_t, vone)


def _mla_kernel(q_ref, k_ref, v_ref, o_ref, s_ref, m_ref, *, seq):
    tq = TQ_MLA
    nb = seq // tq
    nkt = seq // KEY_TILE
    lane = lax.broadcasted_iota(jnp.int32, (tq, LANES), 1)

    def scores(blk, slot):
        r0 = pl.multiple_of(blk * tq, tq)
        for hh in range(2):
            q = q_ref[0, hh, pl.ds(r0, tq), :]
            macc = None
            for kt in range(nkt):
                cols = slice(kt * KEY_TILE, (kt + 1) * KEY_TILE)
                s = _dot_nt(q, k_ref[0, hh, cols, :])
                s_ref[slot, hh, :, cols] = s
                t = jnp.maximum(s[:, :LANES], s[:, LANES:])
                macc = t if macc is None else jnp.maximum(macc, t)
            m = jnp.max(macc, axis=-1, keepdims=True)
            m_ref[slot, hh] = jnp.broadcast_to(m, (tq, LANES))

    def values(blk, slot):
        r0 = pl.multiple_of(blk * tq, tq)
        outs = []
        for hh in range(2):
            acc = None
            for kt in range(nkt):
                cols = slice(kt * KEY_TILE, (kt + 1) * KEY_TILE)
                mb = m_ref[slot, hh]
                p = jnp.exp2(s_ref[slot, hh, :, cols] - jnp.concatenate([mb, mb], axis=1))
                part = _dot(p.astype(BF16), v_ref[0, hh, cols, :])
                acc = part if acc is None else acc + part
            lcol = MLA_V if hh == 0 else 0
            outs.append(acc * (1.0 / acc[:, lcol:lcol + 1]))
        o_ref[0, pl.ds(r0, tq), :] = jnp.where(lane < MLA_V, outs[0], outs[1]).astype(o_ref.dtype)

    scores(0, 0)

    def body(j, carry):
        scores(2 * j + 1, 1)
        values(2 * j, 0)
        scores(2 * j + 2, 0)
        values(2 * j + 1, 1)
        return carry

    lax.fori_loop(0, nb // 2 - 1, body, 0)
    scores(nb - 1, 1)
    values(nb - 2, 0)
    values(nb - 1, 1)


def _mla_call(qm, km, vm):
    batch, _, seq, _ = qm.shape
    in_spec = pl.BlockSpec((1, 2, seq, LANES), lambda b, g: (b, g, 0, 0))
    return pl.pallas_call(
        functools.partial(_mla_kernel, seq=seq),
        grid=(batch, MLA_HEADS // 2),
        in_specs=[in_spec, in_spec, in_spec],
        out_specs=pl.BlockSpec((1, seq, LANES), lambda b, g: (b, 0, g)),
        out_shape=jax.ShapeDtypeStruct((batch, seq, MLA_HEADS * MLA_V), BF16),
        scratch_shapes=[pltpu.VMEM((2, 2, TQ_MLA, seq), F32),
                        pltpu.VMEM((2, 2, TQ_MLA, LANES), F32)],
        compiler_params=pltpu.CompilerParams(
            dimension_semantics=("arbitrary", "arbitrary"), vmem_limit_bytes=VMEM_LIMIT),
        name="mla_attn",
    )(qm, km, vm)


def _diff_kernel(lam_ref, gsub_ref, q_ref, k_ref, v_ref, o_ref, bias_ref, k12_ref, vaug_ref,
                 s_ref, m_ref, *, seq, lam_init):
    tq = TQ_DIFF
    b = pl.program_id(0)
    h = pl.program_id(1)
    wide = 2 * seq - tq

    @pl.when(b == 0)
    def _():
        r = lax.broadcasted_iota(jnp.int32, (tq, wide), 0)
        c = lax.broadcasted_iota(jnp.int32, (tq, wide), 1)
        dist = jnp.abs(r - (c - (seq - tq))).astype(F32)
        hv = jnp.full((1, 1), h, jnp.int32)
        slope = jnp.zeros((1, 1), F32)
        for hh in range(DIFF_HEADS):
            slope = jnp.where(hv == hh, 2.0 ** (-8.0 * (hh + 1.0) / DIFF_HEADS), slope)
        bias_ref[h] = dist * (-LOG2E * slope)

    lp = lam_ref[...]
    lam = (jnp.exp(jnp.sum(lp[0:1] * lp[1:2], axis=-1, keepdims=True))
           - jnp.exp(jnp.sum(lp[2:3] * lp[3:4], axis=-1, keepdims=True)) + lam_init)

    k = k_ref[0, 0]
    lane_k = lax.broadcasted_iota(jnp.int32, k.shape, 1)
    zero = jnp.zeros_like(k)
    k12_ref[0] = jnp.where(lane_k < DIFF_D, k, zero)
    k12_ref[1] = jnp.where(lane_k >= DIFF_D, k, zero)
    vaug_ref[:, :LANES] = v_ref[0, 0]
    vaug_ref[:, LANES:] = (lane_k == 0).astype(BF16)
    g = gsub_ref[...] * (1.0 - lam_init)

    nb = seq // tq
    nkt = seq // KEY_TILE

    def scores(blk, slot):
        r0 = pl.multiple_of(blk * tq, tq)
        q = q_ref[0, 0, pl.ds(r0, tq), :]
        for mi in range(2):
            macc = None
            for kt in range(nkt):
                cols = slice(kt * KEY_TILE, (kt + 1) * KEY_TILE)
                c0 = pl.multiple_of((seq - tq) - blk * tq + kt * KEY_TILE, LANES)
                s = _dot_nt(q, k12_ref[mi, cols, :]) + bias_ref[h, :, pl.ds(c0, KEY_TILE)]
                s_ref[slot, mi, :, cols] = s
                t = jnp.maximum(s[:, :LANES], s[:, LANES:])
                macc = t if macc is None else jnp.maximum(macc, t)
            m = jnp.max(macc, axis=-1, keepdims=True)
            m_ref[slot, mi] = jnp.broadcast_to(m, (tq, LANES))

    def values(blk, slot):
        r0 = pl.multiple_of(blk * tq, tq)
        outs = []
        for mi in range(2):
            acc = None
            for kt in range(nkt):
                cols = slice(kt * KEY_TILE, (kt + 1) * KEY_TILE)
                mb = m_ref[slot, mi]
                p = jnp.exp2(s_ref[slot, mi, :, cols] - jnp.concatenate([mb, mb], axis=1))
                part = _dot(p.astype(BF16), vaug_ref[cols, :])
                acc = part if acc is None else acc + part
            outs.append(acc[:, :LANES] * (1.0 / acc[:, LANES:LANES + 1]))
        o = outs[0] - lam * outs[1]
        o_ref[0, pl.ds(r0, tq), :] = _rms(o, g, SUBLN_EPS).astype(o_ref.dtype)

    scores(0, 0)

    def body(j, carry):
        scores(2 * j + 1, 1)
        values(2 * j, 0)
        scores(2 * j + 2, 0)
        values(2 * j + 1, 1)
        return carry

    lax.fori_loop(0, nb // 2 - 1, body, 0)
    scores(nb - 1, 1)
    values(nb - 2, 0)
    values(nb - 1, 1)


def _diff_call(lam_p, gsub, dq, dk, dv, lam_init):
    batch, _, seq, _ = dq.shape
    in_spec = pl.BlockSpec((1, 1, seq, LANES), lambda b, h: (b, h, 0, 0))
    return pl.pallas_call(
        functools.partial(_diff_kernel, seq=seq, lam_init=lam_init),
        grid=(batch, DIFF_HEADS),
        in_specs=[_const_spec(lam_p.shape), _const_spec(gsub.shape), in_spec, in_spec, in_spec],
        out_specs=pl.BlockSpec((1, seq, LANES), lambda b, h: (b, 0, h)),
        out_shape=jax.ShapeDtypeStruct((batch, seq, DIFF_HEADS * LANES), BF16),
        scratch_shapes=[pltpu.VMEM((DIFF_HEADS, TQ_DIFF, 2 * seq - TQ_DIFF), F32),
                        pltpu.VMEM((2, seq, LANES), BF16),
                        pltpu.VMEM((seq, 2 * LANES), BF16),
                        pltpu.VMEM((2, 2, TQ_DIFF, seq), F32),
                        pltpu.VMEM((2, 2, TQ_DIFF, LANES), F32)],
        compiler_params=pltpu.CompilerParams(
            dimension_semantics=("arbitrary", "arbitrary"), vmem_limit_bytes=VMEM_LIMIT),
        name="diff_attn",
    )(lam_p, gsub, dq, dk, dv)


def _mlp_kernel(om_ref, od_ref, h_ref, wo1_ref, wo2_ref, gmlp_ref, wup_ref, wdn_ref, gfin_ref,
                out_ref, *, final):
    a = h_ref[...] + _dot(om_ref[...], wo1_ref[...]) + _dot(od_ref[...], wo2_ref[...])
    u = _rms(a, gmlp_ref[...], EPS).astype(BF16)
    out_ref[...] = a
    for c in range(D_FF // FF_CHUNK):
        f = jnp.maximum(_dot(u, wup_ref[:, c * FF_CHUNK:(c + 1) * FF_CHUNK]), 0.0)
        out_ref[...] += _dot((f * f).astype(BF16), wdn_ref[c * FF_CHUNK:(c + 1) * FF_CHUNK, :])
    if final:
        out_ref[...] = _rms(out_ref[...], gfin_ref[...], EPS)


def _mlp_call(om, od, hf, wo1, wo2, gmlp, wup, wdn, gfin, final):
    tm = TM_MLP
    tokens = hf.shape[0]
    row = lambda w: pl.BlockSpec((tm, w), lambda i: (i, 0))
    return pl.pallas_call(
        functools.partial(_mlp_kernel, final=final),
        grid=(tokens // tm,),
        in_specs=[row(om.shape[1]), row(od.shape[1]), row(D_MODEL),
                  _const_spec(wo1.shape), _const_spec(wo2.shape), _const_spec(gmlp.shape),
                  _const_spec(wup.shape), _const_spec(wdn.shape), _const_spec(gfin.shape)],
        out_specs=row(D_MODEL),
        out_shape=jax.ShapeDtypeStruct(hf.shape, F32),
        compiler_params=pltpu.CompilerParams(
            dimension_semantics=("arbitrary",), vmem_limit_bytes=VMEM_LIMIT),
        name="outproj_mlp",
    )(om, od, hf, wo1, wo2, gmlp, wup, wdn, gfin)


def _pack_w_in(w):
    zeros = lambda n: jnp.zeros((D_MODEL, n), w.dtype)
    off_kr = Q_LORA + KV_LORA
    off_dq = off_kr + MLA_ROPE
    kr = jnp.concatenate([zeros(MLA_NOPE), w[:, off_kr:off_dq], zeros(LANES - MLA_NOPE - MLA_ROPE)], 1)
    return jnp.concatenate([w[:, :off_kr], kr, w[:, off_dq:]], axis=1).astype(BF16)


def _pack_w_uq(w):
    hd = MLA_NOPE + MLA_ROPE
    w = w.reshape(Q_LORA, MLA_HEADS, hd)
    w = jnp.pad(w, ((0, 0), (0, 0), (0, LANES - hd)))
    return w.reshape(Q_LORA, MLA_HEADS * LANES).astype(BF16)


def _pack_w_ukv(w):
    w = w.reshape(KV_LORA, MLA_HEADS, MLA_NOPE + MLA_V)
    wk = jnp.pad(w[:, :, :MLA_NOPE], ((0, 0), (0, 0), (0, LANES - MLA_NOPE)))
    wv = w[:, :, MLA_NOPE:]
    pad = jnp.zeros_like(wv)
    wv_even = jnp.concatenate([wv, pad], axis=-1)
    wv_odd = jnp.concatenate([pad, wv], axis=-1)
    odd = (jnp.arange(MLA_HEADS) % 2 == 1)[None, :, None]
    wv = jnp.where(odd, wv_odd, wv_even)
    return jnp.concatenate([wk.reshape(KV_LORA, -1), wv.reshape(KV_LORA, -1)], axis=1).astype(BF16)


def _rope_tables(seq):
    pos = np.arange(seq, dtype=np.float32)
    inv_freq = (1.0 / (ROPE_THETA ** (np.arange(0, MLA_ROPE, 2, dtype=np.float32) / MLA_ROPE))).astype(np.float32)
    ang = jnp.asarray(pos[:, None] * inv_freq[None, :])
    cos, sin = jnp.cos(ang), jnp.sin(ang)
    half = MLA_ROPE // 2
    ones = jnp.ones((seq, MLA_NOPE), F32)
    z = lambda n: jnp.zeros((seq, n), F32)
    tail = LANES - MLA_NOPE - MLA_ROPE
    cos_t = jnp.concatenate([ones, cos, cos, z(tail)], axis=1)
    sa_t = jnp.concatenate([z(MLA_NOPE), -sin, z(half), z(tail)], axis=1)
    sb_t = jnp.concatenate([z(MLA_NOPE), z(half), sin, z(tail)], axis=1)
    return cos_t, sa_t, sb_t


def _v_ones():
    v = np.zeros((1, MLA_HEADS * LANES), np.float32)
    for h in range(MLA_HEADS):
        v[0, h * LANES + (MLA_V if h % 2 == 0 else 0)] = 1.0
    return jnp.asarray(v)


def kernel(x, w_in, g_mix, g_q, g_kv, w_uq, w_ukv, lam_q1, lam_k1, lam_q2, lam_k2,
           g_sub, w_out, g_mlp, w_up, w_down, g_final):
    batch, seq, _ = x.shape
    depth = w_in.shape[0]
    cos_t, sa_t, sb_t = _rope_tables(seq)
    vone = _v_ones()
    hf = x.reshape(batch * seq, D_MODEL)
    for l in range(depth):
        lam_init = 0.8 - 0.6 * math.exp(-0.3 * l)
        qm, km, vm, dq, dk, dv = _proj_call(
            hf, g_mix[l][None], _pack_w_in(w_in[l]), g_q[l][None], g_kv[l][None],
            _pack_w_uq(w_uq[l]), _pack_w_ukv(w_ukv[l]), cos_t, sa_t, sb_t, vone, batch, seq)
        o_mla = _mla_call(qm, km, vm)
        lam_p = jnp.stack([lam_q1[l], lam_k1[l], lam_q2[l], lam_k2[l]]).astype(F32)
        o_diff = _diff_call(lam_p, g_sub[l][None], dq, dk, dv, lam_init)
        wo = w_out[l].astype(BF16)
        n_mla = MLA_HEADS * MLA_V
        hf = _mlp_call(o_mla.reshape(batch * seq, -1), o_diff.reshape(batch * seq, -1), hf,
                       wo[:n_mla], wo[n_mla:], g_mlp[l][None], w_up[l].astype(BF16),
                       w_down[l].astype(BF16), g_final[None], final=(l == depth - 1))
    return hf.reshape(batch, seq, D_MODEL)
```

```python
import functools
import math

import numpy as np
import jax
import jax.numpy as jnp
from jax import lax
from jax.experimental import pallas as pl
from jax.experimental.pallas import tpu as pltpu

D_MODEL = 1024
MLA_HEADS = 8
MLA_NOPE = 64
MLA_ROPE = 32
MLA_V = 64
Q_LORA = 256
KV_LORA = 256
DIFF_HEADS = 4
DIFF_D = 64
D_FF = 4 * D_MODEL
ROPE_THETA = 10000.0
EPS = 1e-6
SUBLN_EPS = 1e-5

LANES = 128
LOG2E = math.log2(math.e)

C_CQ = 0
C_CKV = C_CQ + Q_LORA
C_KR = C_CKV + KV_LORA
C_DQ = C_KR + LANES
C_DK = C_DQ + DIFF_HEADS * LANES
C_DV = C_DK + DIFF_HEADS * LANES
P_PACK = C_DV + DIFF_HEADS * LANES

TM_PROJ = 512
TM_MLP = 512
TQ_MLA = 256
TQ_DIFF = 256
FF_CHUNK = 1024
KEY_TILE = 256
VMEM_LIMIT = 56 * 1024 * 1024

F32 = jnp.float32
BF16 = jnp.bfloat16


def _rms(x, g, eps):
    return x * lax.rsqrt(jnp.mean(x * x, axis=-1, keepdims=True) + eps) * g


def _dot(a, b):
    return jnp.dot(a, b, preferred_element_type=F32)


def _aligned(x, m):
    return x if isinstance(x, int) else pl.multiple_of(x, m)


def _dot_nt(a, b):
    return lax.dot_general(a, b, (((1,), (1,)), ((), ())), preferred_element_type=F32)


def _proj_kernel(x_ref, gmix_ref, w1_ref, gq_ref, gkv_ref, wq_ref, wkv_ref,
                 cos_ref, sa_ref, sb_ref, vone_ref,
                 qm_ref, km_ref, vm_ref, dq_ref, dk_ref, dv_ref):
    x = x_ref[...]
    u = _rms(x, gmix_ref[...], EPS).astype(BF16)
    z = _dot(u, w1_ref[...])
    cq = _rms(z[:, C_CQ:C_CKV], gq_ref[...], EPS).astype(BF16)
    ckv = _rms(z[:, C_CKV:C_KR], gkv_ref[...], EPS).astype(BF16)
    q = _dot(cq, wq_ref[...])
    kv = _dot(ckv, wkv_ref[...])
    cos = cos_ref[...]
    sa = sa_ref[...]
    sb = sb_ref[...]

    def rope(t):
        return t * cos + pltpu.roll(t, LANES - 16, 1) * sa + pltpu.roll(t, 16, 1) * sb

    k_rope = rope(z[:, C_KR:C_DQ])
    q_scale = (MLA_NOPE + MLA_ROPE) ** -0.5 * LOG2E
    nk = MLA_HEADS * LANES
    v_all = kv[:, nk:] + vone_ref[...]
    for h in range(MLA_HEADS):
        sl = slice(h * LANES, (h + 1) * LANES)
        qm_ref[0, h] = (rope(q[:, sl]) * q_scale).astype(BF16)
        km_ref[0, h] = (kv[:, sl] + k_rope).astype(BF16)
        vm_ref[0, h] = v_all[:, sl].astype(BF16)
    d_scale = DIFF_D ** -0.5 * LOG2E
    for h in range(DIFF_HEADS):
        dq_ref[0, h] = (z[:, C_DQ + h * LANES:C_DQ + (h + 1) * LANES] * d_scale).astype(BF16)
        dk_ref[0, h] = z[:, C_DK + h * LANES:C_DK + (h + 1) * LANES].astype(BF16)
        dv_ref[0, h] = z[:, C_DV + h * LANES:C_DV + (h + 1) * LANES].astype(BF16)


def _const_spec(shape):
    nd = len(shape)
    return pl.BlockSpec(shape, lambda *_: (0,) * nd, pipeline_mode=pl.Buffered(1))


def _proj_call(xf, gmix, w1, gq, gkv, wq, wkv, cos_t, sa_t, sb_t, vone, batch, seq):
    tm = TM_PROJ
    nsb = seq // tm
    tokens = batch * seq
    head_spec = lambda nh: pl.BlockSpec((1, nh, tm, LANES), lambda i: (i // nsb, 0, i % nsb, 0))
    tab_spec = pl.BlockSpec((tm, LANES), lambda i: (i % nsb, 0))
    mshape = jax.ShapeDtypeStruct((batch, MLA_HEADS, seq, LANES), BF16)
    dshape = jax.ShapeDtypeStruct((batch, DIFF_HEADS, seq, LANES), BF16)
    return pl.pallas_call(
        _proj_kernel,
        grid=(tokens // tm,),
        in_specs=[
            pl.BlockSpec((tm, D_MODEL), lambda i: (i, 0)),
            _const_spec(gmix.shape), _const_spec(w1.shape), _const_spec(gq.shape),
            _const_spec(gkv.shape), _const_spec(wq.shape), _const_spec(wkv.shape),
            tab_spec, tab_spec, tab_spec, _const_spec(vone.shape),
        ],
        out_specs=[head_spec(MLA_HEADS)] * 3 + [head_spec(DIFF_HEADS)] * 3,
        out_shape=[mshape] * 3 + [dshape] * 3,
        compiler_params=pltpu.CompilerParams(
            dimension_semantics=("arbitrary",), vmem_limit_bytes=VMEM_LIMIT),
        name="proj",
    )(xf, gmix, w1, gq, gkv, wq, wkv, cos_t, sa_t, sb_t, vone)


def _mla_kernel(q_ref, k_ref, v_ref, o_ref, s_ref, m_ref, *, seq):
    tq = TQ_MLA
    nb = seq // tq
    nkt = seq // KEY_TILE
    lane = lax.broadcasted_iota(jnp.int32, (tq, LANES), 1)

    def scores(blk, slot):
        r0 = blk * tq
        for hh in range(2):
            q = q_ref[0, hh, pl.ds(r0, tq), :]
            macc = None
            for kt in range(nkt):
                cols = slice(kt * KEY_TILE, (kt + 1) * KEY_TILE)
                s = _dot_nt(q, k_ref[0, hh, cols, :])
                s_ref[slot, hh, :, cols] = s
                t = jnp.maximum(s[:, :LANES], s[:, LANES:])
                macc = t if macc is None else jnp.maximum(macc, t)
            m = jnp.max(macc, axis=-1, keepdims=True)
            m_ref[slot, hh] = jnp.broadcast_to(m, (tq, LANES))

    def values(blk, slot):
        r0 = blk * tq
        outs = []
        for hh in range(2):
            acc = None
            for kt in range(nkt):
                cols = slice(kt * KEY_TILE, (kt + 1) * KEY_TILE)
                mb = m_ref[slot, hh]
                p = jnp.exp2(s_ref[slot, hh, :, cols] - jnp.concatenate([mb, mb], axis=1))
                part = _dot(p.astype(BF16), v_ref[0, hh, cols, :])
                acc = part if acc is None else acc + part
            lcol = MLA_V if hh == 0 else 0
            outs.append(acc * (1.0 / acc[:, lcol:lcol + 1]))
        o_ref[0, pl.ds(r0, tq), :] = jnp.where(lane < MLA_V, outs[0], outs[1]).astype(o_ref.dtype)

    scores(0, 0)
    for blk in range(nb):
        if blk + 1 < nb:
            scores(blk + 1, (blk + 1) % 2)
        values(blk, blk % 2)


def _mla_call(qm, km, vm):
    batch, _, seq, _ = qm.shape
    in_spec = pl.BlockSpec((1, 2, seq, LANES), lambda b, g: (b, g, 0, 0))
    return pl.pallas_call(
        functools.partial(_mla_kernel, seq=seq),
        grid=(batch, MLA_HEADS // 2),
        in_specs=[in_spec, in_spec, in_spec],
        out_specs=pl.BlockSpec((1, seq, LANES), lambda b, g: (b, 0, g)),
        out_shape=jax.ShapeDtypeStruct((batch, seq, MLA_HEADS * MLA_V), BF16),
        scratch_shapes=[pltpu.VMEM((2, 2, TQ_MLA, seq), F32),
                        pltpu.VMEM((2, 2, TQ_MLA, LANES), F32)],
        compiler_params=pltpu.CompilerParams(
            dimension_semantics=("arbitrary", "arbitrary"), vmem_limit_bytes=VMEM_LIMIT),
        name="mla_attn",
    )(qm, km, vm)


def _diff_kernel(lam_ref, gsub_ref, q_ref, k_ref, v_ref, o_ref, bias_ref, k12_ref, vaug_ref,
                 s_ref, m_ref, *, seq, lam_init):
    tq = TQ_DIFF
    b = pl.program_id(0)
    h = pl.program_id(1)
    wide = 2 * seq - tq

    @pl.when(b == 0)
    def _():
        r = lax.broadcasted_iota(jnp.int32, (tq, wide), 0)
        c = lax.broadcasted_iota(jnp.int32, (tq, wide), 1)
        dist = jnp.abs(r - (c - (seq - tq))).astype(F32)
        hv = jnp.full((1, 1), h, jnp.int32)
        slope = jnp.zeros((1, 1), F32)
        for hh in range(DIFF_HEADS):
            slope = jnp.where(hv == hh, 2.0 ** (-8.0 * (hh + 1.0) / DIFF_HEADS), slope)
        bias_ref[h] = dist * (-LOG2E * slope)

    lp = lam_ref[...]
    lam = (jnp.exp(jnp.sum(lp[0:1] * lp[1:2], axis=-1, keepdims=True))
           - jnp.exp(jnp.sum(lp[2:3] * lp[3:4], axis=-1, keepdims=True)) + lam_init)

    k = k_ref[0, 0]
    lane_k = lax.broadcasted_iota(jnp.int32, k.shape, 1)
    zero = jnp.zeros_like(k)
    k12_ref[0] = jnp.where(lane_k < DIFF_D, k, zero)
    k12_ref[1] = jnp.where(lane_k >= DIFF_D, k, zero)
    vaug_ref[:, :LANES] = v_ref[0, 0]
    vaug_ref[:, LANES:] = (lane_k == 0).astype(BF16)
    g = gsub_ref[...] * (1.0 - lam_init)

    nb = seq // tq
    nkt = seq // KEY_TILE

    def scores(blk, slot):
        q = q_ref[0, 0, pl.ds(_aligned(blk * tq, tq), tq), :]
        for mi in range(2):
            macc = None
            for kt in range(nkt):
                cols = slice(kt * KEY_TILE, (kt + 1) * KEY_TILE)
                c0 = _aligned((seq - tq) - blk * tq + kt * KEY_TILE, LANES)
                s = _dot_nt(q, k12_ref[mi, cols, :]) + bias_ref[h, :, pl.ds(c0, KEY_TILE)]
                s_ref[slot, mi, :, cols] = s
                t = jnp.maximum(s[:, :LANES], s[:, LANES:])
                macc = t if macc is None else jnp.maximum(macc, t)
            m = jnp.max(macc, axis=-1, keepdims=True)
            m_ref[slot, mi] = jnp.broadcast_to(m, (tq, LANES))

    def values(blk, slot):
        outs = []
        for mi in range(2):
            acc = None
            for kt in range(nkt):
                cols = slice(kt * KEY_TILE, (kt + 1) * KEY_TILE)
                mb = m_ref[slot, mi]
                p = jnp.exp2(s_ref[slot, mi, :, cols] - jnp.concatenate([mb, mb], axis=1))
                part = _dot(p.astype(BF16), vaug_ref[cols, :])
                acc = part if acc is None else acc + part
            outs.append(acc[:, :LANES] * (1.0 / acc[:, LANES:LANES + 1]))
        o = outs[0] - lam * outs[1]
        r0 = _aligned(blk * tq, tq)
        o_ref[0, pl.ds(r0, tq), :] = _rms(o, g, SUBLN_EPS).astype(o_ref.dtype)

    scores(0, 0)
    scores(1, 1)
    values(0, 0)

    def body(j, carry):
        for u in range(3):
            t = 3 * j + 2 + u
            scores(t, (2 + u) % 3)
            values(t - 1, (1 + u) % 3)
        return carry

    lax.fori_loop(0, (nb - 2) // 3, body, 0)
    values(nb - 1, (nb - 1) % 3)


def _diff_call(lam_p, gsub, dq, dk, dv, lam_init):
    batch, _, seq, _ = dq.shape
    in_spec = pl.BlockSpec((1, 1, seq, LANES), lambda b, h: (b, h, 0, 0))
    return pl.pallas_call(
        functools.partial(_diff_kernel, seq=seq, lam_init=lam_init),
        grid=(batch, DIFF_HEADS),
        in_specs=[_const_spec(lam_p.shape), _const_spec(gsub.shape), in_spec, in_spec, in_spec],
        out_specs=pl.BlockSpec((1, seq, LANES), lambda b, h: (b, 0, h)),
        out_shape=jax.ShapeDtypeStruct((batch, seq, DIFF_HEADS * LANES), BF16),
        scratch_shapes=[pltpu.VMEM((DIFF_HEADS, TQ_DIFF, 2 * seq - TQ_DIFF), F32),
                        pltpu.VMEM((2, seq, LANES), BF16),
                        pltpu.VMEM((seq, 2 * LANES), BF16),
                        pltpu.VMEM((3, 2, TQ_DIFF, seq), F32),
                        pltpu.VMEM((3, 2, TQ_DIFF, LANES), F32)],
        compiler_params=pltpu.CompilerParams(
            dimension_semantics=("arbitrary", "arbitrary"), vmem_limit_bytes=VMEM_LIMIT),
        name="diff_attn",
    )(lam_p, gsub, dq, dk, dv)


def _mlp_kernel(om_ref, od_ref, h_ref, wo1_ref, wo2_ref, gmlp_ref, wup_ref, wdn_ref, gfin_ref,
                out_ref, *, final):
    a = h_ref[...] + _dot(om_ref[...], wo1_ref[...]) + _dot(od_ref[...], wo2_ref[...])
    u = _rms(a, gmlp_ref[...], EPS).astype(BF16)
    out_ref[...] = a
    for c in range(D_FF // FF_CHUNK):
        f = jnp.maximum(_dot(u, wup_ref[:, c * FF_CHUNK:(c + 1) * FF_CHUNK]), 0.0)
        out_ref[...] += _dot((f * f).astype(BF16), wdn_ref[c * FF_CHUNK:(c + 1) * FF_CHUNK, :])
    if final:
        out_ref[...] = _rms(out_ref[...], gfin_ref[...], EPS)


def _mlp_call(om, od, hf, wo1, wo2, gmlp, wup, wdn, gfin, final):
    tm = TM_MLP
    tokens = hf.shape[0]
    row = lambda w: pl.BlockSpec((tm, w), lambda i: (i, 0))
    return pl.pallas_call(
        functools.partial(_mlp_kernel, final=final),
        grid=(tokens // tm,),
        in_specs=[row(om.shape[1]), row(od.shape[1]), row(D_MODEL),
                  _const_spec(wo1.shape), _const_spec(wo2.shape), _const_spec(gmlp.shape),
                  _const_spec(wup.shape), _const_spec(wdn.shape), _const_spec(gfin.shape)],
        out_specs=row(D_MODEL),
        out_shape=jax.ShapeDtypeStruct(hf.shape, F32),
        compiler_params=pltpu.CompilerParams(
            dimension_semantics=("arbitrary",), vmem_limit_bytes=VMEM_LIMIT),
        name="outproj_mlp",
    )(om, od, hf, wo1, wo2, gmlp, wup, wdn, gfin)


def _pack_w_in(w):
    zeros = lambda n: jnp.zeros((D_MODEL, n), w.dtype)
    off_kr = Q_LORA + KV_LORA
    off_dq = off_kr + MLA_ROPE
    kr = jnp.concatenate([zeros(MLA_NOPE), w[:, off_kr:off_dq], zeros(LANES - MLA_NOPE - MLA_ROPE)], 1)
    return jnp.concatenate([w[:, :off_kr], kr, w[:, off_dq:]], axis=1).astype(BF16)


def _pack_w_uq(w):
    hd = MLA_NOPE + MLA_ROPE
    w = w.reshape(Q_LORA, MLA_HEADS, hd)
    w = jnp.pad(w, ((0, 0), (0, 0), (0, LANES - hd)))
    return w.reshape(Q_LORA, MLA_HEADS * LANES).astype(BF16)


def _pack_w_ukv(w):
    w = w.reshape(KV_LORA, MLA_HEADS, MLA_NOPE + MLA_V)
    wk = jnp.pad(w[:, :, :MLA_NOPE], ((0, 0), (0, 0), (0, LANES - MLA_NOPE)))
    wv = w[:, :, MLA_NOPE:]
    pad = jnp.zeros_like(wv)
    wv_even = jnp.concatenate([wv, pad], axis=-1)
    wv_odd = jnp.concatenate([pad, wv], axis=-1)
    odd = (jnp.arange(MLA_HEADS) % 2 == 1)[None, :, None]
    wv = jnp.where(odd, wv_odd, wv_even)
    return jnp.concatenate([wk.reshape(KV_LORA, -1), wv.reshape(KV_LORA, -1)], axis=1).astype(BF16)


def _rope_tables(seq):
    pos = np.arange(seq, dtype=np.float32)
    inv_freq = (1.0 / (ROPE_THETA ** (np.arange(0, MLA_ROPE, 2, dtype=np.float32) / MLA_ROPE))).astype(np.float32)
    ang = jnp.asarray(pos[:, None] * inv_freq[None, :])
    cos, sin = jnp.cos(ang), jnp.sin(ang)
    half = MLA_ROPE // 2
    ones = jnp.ones((seq, MLA_NOPE), F32)
    z = lambda n: jnp.zeros((seq, n), F32)
    tail = LANES - MLA_NOPE - MLA_ROPE
    cos_t = jnp.concatenate([ones, cos, cos, z(tail)], axis=1)
    sa_t = jnp.concatenate([z(MLA_NOPE), -sin, z(half), z(tail)], axis=1)
    sb_t = jnp.concatenate([z(MLA_NOPE), z(half), sin, z(tail)], axis=1)
    return cos_t, sa_t, sb_t


def _v_ones():
    v = np.zeros((1, MLA_HEADS * LANES), np.float32)
    for h in range(MLA_HEADS):
        v[0, h * LANES + (MLA_V if h % 2 == 0 else 0)] = 1.0
    return jnp.asarray(v)


def kernel(x, w_in, g_mix, g_q, g_kv, w_uq, w_ukv, lam_q1, lam_k1, lam_q2, lam_k2,
           g_sub, w_out, g_mlp, w_up, w_down, g_final):
    batch, seq, _ = x.shape
    depth = w_in.shape[0]
    cos_t, sa_t, sb_t = _rope_tables(seq)
    vone = _v_ones()
    hf = x.reshape(batch * seq, D_MODEL)
    for l in range(depth):
        lam_init = 0.8 - 0.6 * math.exp(-0.3 * l)
        qm, km, vm, dq, dk, dv = _proj_call(
            hf, g_mix[l][None], _pack_w_in(w_in[l]), g_q[l][None], g_kv[l][None],
            _pack_w_uq(w_uq[l]), _pack_w_ukv(w_ukv[l]), cos_t, sa_t, sb_t, vone, batch, seq)
        o_mla = _mla_call(qm, km, vm)
        lam_p = jnp.stack([lam_q1[l], lam_k1[l], lam_q2[l], lam_k2[l]]).astype(F32)
        o_diff = _diff_call(lam_p, g_sub[l][None], dq, dk, dv, lam_init)
        wo = w_out[l].astype(BF16)
        n_mla = MLA_HEADS * MLA_V
        hf = _mlp_call(o_mla.reshape(batch * seq, -1), o_diff.reshape(batch * seq, -1), hf,
                       wo[:n_mla], wo[n_mla:], g_mlp[l][None], w_up[l].astype(BF16),
                       w_down[l].astype(BF16), g_final[None], final=(l == depth - 1))
    return hf.reshape(batch, seq, D_MODEL)
```

```python
import functools
import math

import numpy as np
import jax
import jax.numpy as jnp
from jax import lax
from jax.experimental import pallas as pl
from jax.experimental.pallas import tpu as pltpu

D_MODEL = 1024
MLA_HEADS = 8
MLA_NOPE = 64
MLA_ROPE = 32
MLA_V = 64
Q_LORA = 256
KV_LORA = 256
DIFF_HEADS = 4
DIFF_D = 64
D_FF = 4 * D_MODEL
ROPE_THETA = 10000.0
EPS = 1e-6
SUBLN_EPS = 1e-5

LANES = 128
LOG2E = math.log2(math.e)

C_CQ = 0
C_CKV = C_CQ + Q_LORA
C_KR = C_CKV + KV_LORA
C_DQ = C_KR + LANES
C_DK = C_DQ + DIFF_HEADS * LANES
C_DV = C_DK + DIFF_HEADS * LANES
P_PACK = C_DV + DIFF_HEADS * LANES

TM_PROJ = 512
TM_MLP = 512
TQ_MLA = 256
TQ_DIFF = 256
FF_CHUNK = 1024
KEY_TILE = 256
VMEM_LIMIT = 56 * 1024 * 1024

F32 = jnp.float32
BF16 = jnp.bfloat16


def _rms(x, g, eps):
    return x * lax.rsqrt(jnp.mean(x * x, axis=-1, keepdims=True) + eps) * g


def _dot(a, b):
    return jnp.dot(a, b, preferred_element_type=F32)


def _aligned(x, m):
    return x if isinstance(x, int) else pl.multiple_of(x, m)


def _dot_nt(a, b):
    return lax.dot_general(a, b, (((1,), (1,)), ((), ())), preferred_element_type=F32)


def _proj_kernel(x_ref, gmix_ref, w1_ref, gq_ref, gkv_ref, wq_ref, wkv_ref,
                 cos_ref, sa_ref, sb_ref, vone_ref,
                 qm_ref, km_ref, vm_ref, dq_ref, dk_ref, dv_ref):
    x = x_ref[...]
    u = _rms(x, gmix_ref[...], EPS).astype(BF16)
    z = _dot(u, w1_ref[...])
    cq = _rms(z[:, C_CQ:C_CKV], gq_ref[...], EPS).astype(BF16)
    ckv = _rms(z[:, C_CKV:C_KR], gkv_ref[...], EPS).astype(BF16)
    q = _dot(cq, wq_ref[...])
    kv = _dot(ckv, wkv_ref[...])
    cos = cos_ref[...]
    sa = sa_ref[...]
    sb = sb_ref[...]

    def rope(t):
        return t * cos + pltpu.roll(t, LANES - 16, 1) * sa + pltpu.roll(t, 16, 1) * sb

    k_rope = rope(z[:, C_KR:C_DQ])
    q_scale = (MLA_NOPE + MLA_ROPE) ** -0.5 * LOG2E
    nk = MLA_HEADS * LANES
    v_all = kv[:, nk:] + vone_ref[...]
    for h in range(MLA_HEADS):
        sl = slice(h * LANES, (h + 1) * LANES)
        qm_ref[0, h] = (rope(q[:, sl]) * q_scale).astype(BF16)
        km_ref[0, h] = (kv[:, sl] + k_rope).astype(BF16)
        vm_ref[0, h] = v_all[:, sl].astype(BF16)
    d_scale = DIFF_D ** -0.5 * LOG2E
    for h in range(DIFF_HEADS):
        dq_ref[0, h] = (z[:, C_DQ + h * LANES:C_DQ + (h + 1) * LANES] * d_scale).astype(BF16)
        dk_ref[0, h] = z[:, C_DK + h * LANES:C_DK + (h + 1) * LANES].astype(BF16)
        dv_ref[0, h] = z[:, C_DV + h * LANES:C_DV + (h + 1) * LANES].astype(BF16)


def _const_spec(shape):
    nd = len(shape)
    return pl.BlockSpec(shape, lambda *_: (0,) * nd, pipeline_mode=pl.Buffered(1))


def _proj_call(xf, gmix, w1, gq, gkv, wq, wkv, cos_t, sa_t, sb_t, vone, batch, seq):
    tm = TM_PROJ
    nsb = seq // tm
    tokens = batch * seq
    head_spec = lambda nh: pl.BlockSpec((1, nh, tm, LANES), lambda i: (i // nsb, 0, i % nsb, 0))
    tab_spec = pl.BlockSpec((tm, LANES), lambda i: (i % nsb, 0))
    mshape = jax.ShapeDtypeStruct((batch, MLA_HEADS, seq, LANES), BF16)
    dshape = jax.ShapeDtypeStruct((batch, DIFF_HEADS, seq, LANES), BF16)
    return pl.pallas_call(
        _proj_kernel,
        grid=(tokens // tm,),
        in_specs=[
            pl.BlockSpec((tm, D_MODEL), lambda i: (i, 0)),
            _const_spec(gmix.shape), _const_spec(w1.shape), _const_spec(gq.shape),
            _const_spec(gkv.shape), _const_spec(wq.shape), _const_spec(wkv.shape),
            tab_spec, tab_spec, tab_spec, _const_spec(vone.shape),
        ],
        out_specs=[head_spec(MLA_HEADS)] * 3 + [head_spec(DIFF_HEADS)] * 3,
        out_shape=[mshape] * 3 + [dshape] * 3,
        compiler_params=pltpu.CompilerParams(
            dimension_semantics=("arbitrary",), vmem_limit_bytes=VMEM_LIMIT),
        name="proj",
    )(xf, gmix, w1, gq, gkv, wq, wkv, cos_t, sa_t, sb_t, vone)


def _mla_kernel(q_ref, k_ref, v_ref, o_ref, s_ref, m_ref, *, seq):
    tq = TQ_MLA
    nb = seq // tq
    nkt = seq // KEY_TILE
    lane = lax.broadcasted_iota(jnp.int32, (tq, LANES), 1)

    def scores(blk, slot):
        r0 = blk * tq
        for hh in range(2):
            q = q_ref[0, hh, pl.ds(r0, tq), :]
            macc = None
            for kt in range(nkt):
                cols = slice(kt * KEY_TILE, (kt + 1) * KEY_TILE)
                s = _dot_nt(q, k_ref[0, hh, cols, :])
                s_ref[slot, hh, :, cols] = s
                t = jnp.maximum(s[:, :LANES], s[:, LANES:])
                macc = t if macc is None else jnp.maximum(macc, t)
            m = jnp.max(macc, axis=-1, keepdims=True)
            m_ref[slot, hh] = jnp.broadcast_to(m, (tq, LANES))

    def values(blk, slot):
        r0 = blk * tq
        outs = []
        for hh in range(2):
            acc = None
            for kt in range(nkt):
                cols = slice(kt * KEY_TILE, (kt + 1) * KEY_TILE)
                mb = m_ref[slot, hh]
                p = jnp.exp2(s_ref[slot, hh, :, cols] - jnp.concatenate([mb, mb], axis=1))
                part = _dot(p.astype(BF16), v_ref[0, hh, cols, :])
                acc = part if acc is None else acc + part
            lcol = MLA_V if hh == 0 else 0
            outs.append(acc * (1.0 / acc[:, lcol:lcol + 1]))
        o_ref[0, pl.ds(r0, tq), :] = jnp.where(lane < MLA_V, outs[0], outs[1]).astype(o_ref.dtype)

    scores(0, 0)
    for blk in range(nb):
        if blk + 1 < nb:
            scores(blk + 1, (blk + 1) % 2)
        values(blk, blk % 2)


def _mla_call(qm, km, vm):
    batch, _, seq, _ = qm.shape
    in_spec = pl.BlockSpec((1, 2, seq, LANES), lambda b, g: (b, g, 0, 0))
    return pl.pallas_call(
        functools.partial(_mla_kernel, seq=seq),
        grid=(batch, MLA_HEADS // 2),
        in_specs=[in_spec, in_spec, in_spec],
        out_specs=pl.BlockSpec((1, seq, LANES), lambda b, g: (b, 0, g)),
        out_shape=jax.ShapeDtypeStruct((batch, seq, MLA_HEADS * MLA_V), BF16),
        scratch_shapes=[pltpu.VMEM((2, 2, TQ_MLA, seq), F32),
                        pltpu.VMEM((2, 2, TQ_MLA, LANES), F32)],
        compiler_params=pltpu.CompilerParams(
            dimension_semantics=("arbitrary", "arbitrary"), vmem_limit_bytes=VMEM_LIMIT),
        name="mla_attn",
    )(qm, km, vm)


def _diff_kernel(lam_ref, gsub_ref, roles_ref, q_ref, k_ref, v_ref, o_ref,
                 bias_ref, kfeat_ref, k12_ref, vaug_ref, s_ref, m_ref, *, seq, lam_init):
    tq = TQ_DIFF
    b = pl.program_id(0)
    h = pl.program_id(1)
    nb = seq // tq
    nkt = seq // KEY_TILE
    assert tq == KEY_TILE

    hv = jnp.full((1, 1), h, jnp.int32)
    slope = jnp.zeros((1, 1), F32)
    for hh in range(DIFF_HEADS):
        slope = jnp.where(hv == hh, 2.0 ** (-8.0 * (hh + 1.0) / DIFF_HEADS), slope)

    @pl.when(b == 0)
    def _():
        r = lax.broadcasted_iota(jnp.int32, (tq, KEY_TILE), 0)
        c = lax.broadcasted_iota(jnp.int32, (tq, KEY_TILE), 1)
        bias_ref[h] = jnp.abs(r - c).astype(F32) * (-LOG2E * slope)
        j = lax.broadcasted_iota(jnp.int32, (seq, LANES), 0)
        j_tile = (j - (j & (KEY_TILE - 1))).astype(F32)
        j_in = (j & (KEY_TILE - 1)).astype(F32)
        for mi in range(2):
            kf = (slope * roles_ref[mi, 0:1] + roles_ref[mi, 1:2] * j_tile
                  + roles_ref[mi, 2:3] * j_in)
            kfeat_ref[h, mi] = kf.astype(BF16)

    lp = lam_ref[...]
    lam = (jnp.exp(jnp.sum(lp[0:1] * lp[1:2], axis=-1, keepdims=True))
           - jnp.exp(jnp.sum(lp[2:3] * lp[3:4], axis=-1, keepdims=True)) + lam_init)

    k = k_ref[0, 0]
    lane_k = lax.broadcasted_iota(jnp.int32, k.shape, 1)
    k12_ref[0] = jnp.where(lane_k < DIFF_D, k, kfeat_ref[h, 0])
    k12_ref[1] = jnp.where(lane_k >= DIFF_D, k, kfeat_ref[h, 1])
    vaug_ref[:, :LANES] = v_ref[0, 0]
    vaug_ref[:, LANES:] = (lane_k == 0).astype(BF16)
    g = gsub_ref[...] * (1.0 - lam_init)
    lane_q = lax.broadcasted_iota(jnp.int32, (tq, LANES), 1)
    i_in = lax.broadcasted_iota(jnp.int32, (tq, LANES), 0).astype(F32)

    def scores(blk, slot):
        q = q_ref[0, 0, blk * tq:(blk + 1) * tq, :]
        for mi in range(2):
            data = (lane_q < DIFF_D) if mi == 0 else (lane_q >= DIFF_D)
            qf = (roles_ref[mi, 3:4] * float(blk * tq) + roles_ref[mi, 4:5] * i_in
                  + slope * roles_ref[mi, 5:6])
            lhs = {0: jnp.where(data, q, jnp.zeros_like(q))}
            if blk > 0:
                lhs[1] = jnp.where(data, q, qf.astype(BF16))
            if blk < nb - 1:
                lhs[-1] = jnp.where(data, q, (-qf).astype(BF16))
            macc = None
            for kt in range(nkt):
                cols = slice(kt * KEY_TILE, (kt + 1) * KEY_TILE)
                s = _dot_nt(lhs[(blk > kt) - (blk < kt)], k12_ref[mi, cols, :])
                if kt == blk:
                    s = s + bias_ref[h]
                s_ref[slot, mi, :, cols] = s
                t = jnp.maximum(s[:, :LANES], s[:, LANES:])
                macc = t if macc is None else jnp.maximum(macc, t)
            m = jnp.max(macc, axis=-1, keepdims=True)
            m_ref[slot, mi] = jnp.broadcast_to(m, (tq, LANES))

    def values(blk, slot):
        outs = []
        for mi in range(2):
            acc = None
            for kt in range(nkt):
                cols = slice(kt * KEY_TILE, (kt + 1) * KEY_TILE)
                mb = m_ref[slot, mi]
                p = jnp.exp2(s_ref[slot, mi, :, cols] - jnp.concatenate([mb, mb], axis=1))
                part = _dot(p.astype(BF16), vaug_ref[cols, :])
                acc = part if acc is None else acc + part
            outs.append(acc[:, :LANES] * (1.0 / acc[:, LANES:LANES + 1]))
        o = outs[0] - lam * outs[1]
        o_ref[0, blk * tq:(blk + 1) * tq, :] = _rms(o, g, SUBLN_EPS).astype(o_ref.dtype)

    scores(0, 0)
    for blk in range(nb):
        if blk + 1 < nb:
            scores(blk + 1, (blk + 1) % 2)
        values(blk, blk % 2)


def _alibi_roles():
    import ml_dtypes
    terms = []
    rest = LOG2E
    for _ in range(3):
        t = float(np.float32(rest).astype(ml_dtypes.bfloat16))
        terms.append(t)
        rest -= t
    roles = np.zeros((2, 6, LANES), np.float32)
    for mi in range(2):
        base = DIFF_D if mi == 0 else 0
        for t, lt in enumerate(terms):
            roles[mi, 0, base + t] = -lt
            roles[mi, 0, base + 3 + t] = -lt
            roles[mi, 1, base + 6 + t] = 1.0
            roles[mi, 2, base + 9 + t] = 1.0
            roles[mi, 3, base + t] = 1.0
            roles[mi, 4, base + 3 + t] = 1.0
            roles[mi, 5, base + 6 + t] = lt
            roles[mi, 5, base + 9 + t] = lt
    return jnp.asarray(roles)


def _diff_call(lam_p, gsub, dq, dk, dv, lam_init):
    batch, _, seq, _ = dq.shape
    roles = _alibi_roles()
    in_spec = pl.BlockSpec((1, 1, seq, LANES), lambda b, h: (b, h, 0, 0))
    return pl.pallas_call(
        functools.partial(_diff_kernel, seq=seq, lam_init=lam_init),
        grid=(batch, DIFF_HEADS),
        in_specs=[_const_spec(lam_p.shape), _const_spec(gsub.shape), _const_spec(roles.shape),
                  in_spec, in_spec, in_spec],
        out_specs=pl.BlockSpec((1, seq, LANES), lambda b, h: (b, 0, h)),
        out_shape=jax.ShapeDtypeStruct((batch, seq, DIFF_HEADS * LANES), BF16),
        scratch_shapes=[pltpu.VMEM((DIFF_HEADS, TQ_DIFF, KEY_TILE), F32),
                        pltpu.VMEM((DIFF_HEADS, 2, seq, LANES), BF16),
                        pltpu.VMEM((2, seq, LANES), BF16),
                        pltpu.VMEM((seq, 2 * LANES), BF16),
                        pltpu.VMEM((2, 2, TQ_DIFF, seq), F32),
                        pltpu.VMEM((2, 2, TQ_DIFF, LANES), F32)],
        compiler_params=pltpu.CompilerParams(
            dimension_semantics=("arbitrary", "arbitrary"), vmem_limit_bytes=VMEM_LIMIT),
        name="diff_attn",
    )(lam_p, gsub, roles, dq, dk, dv)


def _mlp_kernel(om_ref, od_ref, h_ref, wo1_ref, wo2_ref, gmlp_ref, wup_ref, wdn_ref, gfin_ref,
                out_ref, *, final):
    a = h_ref[...] + _dot(om_ref[...], wo1_ref[...]) + _dot(od_ref[...], wo2_ref[...])
    u = _rms(a, gmlp_ref[...], EPS).astype(BF16)
    out_ref[...] = a
    for c in range(D_FF // FF_CHUNK):
        f = jnp.maximum(_dot(u, wup_ref[:, c * FF_CHUNK:(c + 1) * FF_CHUNK]), 0.0)
        out_ref[...] += _dot((f * f).astype(BF16), wdn_ref[c * FF_CHUNK:(c + 1) * FF_CHUNK, :])
    if final:
        out_ref[...] = _rms(out_ref[...], gfin_ref[...], EPS)


def _mlp_call(om, od, hf, wo1, wo2, gmlp, wup, wdn, gfin, final):
    tm = TM_MLP
    tokens = hf.shape[0]
    row = lambda w: pl.BlockSpec((tm, w), lambda i: (i, 0))
    return pl.pallas_call(
        functools.partial(_mlp_kernel, final=final),
        grid=(tokens // tm,),
        in_specs=[row(om.shape[1]), row(od.shape[1]), row(D_MODEL),
                  _const_spec(wo1.shape), _const_spec(wo2.shape), _const_spec(gmlp.shape),
                  _const_spec(wup.shape), _const_spec(wdn.shape), _const_spec(gfin.shape)],
        out_specs=row(D_MODEL),
        out_shape=jax.ShapeDtypeStruct(hf.shape, F32),
        compiler_params=pltpu.CompilerParams(
            dimension_semantics=("arbitrary",), vmem_limit_bytes=VMEM_LIMIT),
        name="outproj_mlp",
    )(om, od, hf, wo1, wo2, gmlp, wup, wdn, gfin)


def _pack_w_in(w):
    zeros = lambda n: jnp.zeros((D_MODEL, n), w.dtype)
    off_kr = Q_LORA + KV_LORA
    off_dq = off_kr + MLA_ROPE
    kr = jnp.concatenate([zeros(MLA_NOPE), w[:, off_kr:off_dq], zeros(LANES - MLA_NOPE - MLA_ROPE)], 1)
    return jnp.concatenate([w[:, :off_kr], kr, w[:, off_dq:]], axis=1).astype(BF16)


def _pack_w_uq(w):
    hd = MLA_NOPE + MLA_ROPE
    w = w.reshape(Q_LORA, MLA_HEADS, hd)
    w = jnp.pad(w, ((0, 0), (0, 0), (0, LANES - hd)))
    return w.reshape(Q_LORA, MLA_HEADS * LANES).astype(BF16)


def _pack_w_ukv(w):
    w = w.reshape(KV_LORA, MLA_HEADS, MLA_NOPE + MLA_V)
    wk = jnp.pad(w[:, :, :MLA_NOPE], ((0, 0), (0, 0), (0, LANES - MLA_NOPE)))
    wv = w[:, :, MLA_NOPE:]
    pad = jnp.zeros_like(wv)
    wv_even = jnp.concatenate([wv, pad], axis=-1)
    wv_odd = jnp.concatenate([pad, wv], axis=-1)
    odd = (jnp.arange(MLA_HEADS) % 2 == 1)[None, :, None]
    wv = jnp.where(odd, wv_odd, wv_even)
    return jnp.concatenate([wk.reshape(KV_LORA, -1), wv.reshape(KV_LORA, -1)], axis=1).astype(BF16)


def _rope_tables(seq):
    pos = np.arange(seq, dtype=np.float32)
    inv_freq = (1.0 / (ROPE_THETA ** (np.arange(0, MLA_ROPE, 2, dtype=np.float32) / MLA_ROPE))).astype(np.float32)
    ang = jnp.asarray(pos[:, None] * inv_freq[None, :])
    cos, sin = jnp.cos(ang), jnp.sin(ang)
    half = MLA_ROPE // 2
    ones = jnp.ones((seq, MLA_NOPE), F32)
    z = lambda n: jnp.zeros((seq, n), F32)
    tail = LANES - MLA_NOPE - MLA_ROPE
    cos_t = jnp.concatenate([ones, cos, cos, z(tail)], axis=1)
    sa_t = jnp.concatenate([z(MLA_NOPE), -sin, z(half), z(tail)], axis=1)
    sb_t = jnp.concatenate([z(MLA_NOPE), z(half), sin, z(tail)], axis=1)
    return cos_t, sa_t, sb_t


def _v_ones():
    v = np.zeros((1, MLA_HEADS * LANES), np.float32)
    for h in range(MLA_HEADS):
        v[0, h * LANES + (MLA_V if h % 2 == 0 else 0)] = 1.0
    return jnp.asarray(v)


def kernel(x, w_in, g_mix, g_q, g_kv, w_uq, w_ukv, lam_q1, lam_k1, lam_q2, lam_k2,
           g_sub, w_out, g_mlp, w_up, w_down, g_final):
    batch, seq, _ = x.shape
    depth = w_in.shape[0]
    cos_t, sa_t, sb_t = _rope_tables(seq)
    vone = _v_ones()
    hf = x.reshape(batch * seq, D_MODEL)
    for l in range(depth):
        lam_init = 0.8 - 0.6 * math.exp(-0.3 * l)
        qm, km, vm, dq, dk, dv = _proj_call(
            hf, g_mix[l][None], _pack_w_in(w_in[l]), g_q[l][None], g_kv[l][None],
            _pack_w_uq(w_uq[l]), _pack_w_ukv(w_ukv[l]), cos_t, sa_t, sb_t, vone, batch, seq)
        o_mla = _mla_call(qm, km, vm)
        lam_p = jnp.stack([lam_q1[l], lam_k1[l], lam_q2[l], lam_k2[l]]).astype(F32)
        o_diff = _diff_call(lam_p, g_sub[l][None], dq, dk, dv, lam_init)
        wo = w_out[l].astype(BF16)
        n_mla = MLA_HEADS * MLA_V
        hf = _mlp_call(o_mla.reshape(batch * seq, -1), o_diff.reshape(batch * seq, -1), hf,
                       wo[:n_mla], wo[n_mla:], g_mlp[l][None], w_up[l].astype(BF16),
                       w_down[l].astype(BF16), g_final[None], final=(l == depth - 1))
    return hf.reshape(batch, seq, D_MODEL)
```

```python
import functools
import math

import ml_dtypes
import numpy as np
import jax
import jax.numpy as jnp
from jax import lax
from jax.experimental import pallas as pl
from jax.experimental.pallas import tpu as pltpu

D_MODEL = 1024
MLA_HEADS = 8
MLA_NOPE = 64
MLA_ROPE = 32
MLA_V = 64
Q_LORA = 256
KV_LORA = 256
DIFF_HEADS = 4
DIFF_D = 64
D_FF = 4 * D_MODEL
ROPE_THETA = 10000.0
EPS = 1e-6
SUBLN_EPS = 1e-5

LANES = 128
LOG2E = math.log2(math.e)

C_CQ = 0
C_CKV = C_CQ + Q_LORA
C_KR = C_CKV + KV_LORA
C_DQ = C_KR + LANES
C_DK = C_DQ + DIFF_HEADS * LANES
C_DV = C_DK + DIFF_HEADS * LANES
P_PACK = C_DV + DIFF_HEADS * LANES

TM_PROJ = 512
TM_MLP = 512
TQ_MLA = 256
TQ_DIFF = 256
FF_CHUNK = 1024
KEY_TILE = 256
VMEM_LIMIT = 56 * 1024 * 1024

F32 = jnp.float32
BF16 = jnp.bfloat16


def _rms(x, g, eps):
    return x * lax.rsqrt(jnp.mean(x * x, axis=-1, keepdims=True) + eps) * g


def _dot(a, b):
    return jnp.dot(a, b, preferred_element_type=F32)


def _dot_nt(a, b):
    return lax.dot_general(a, b, (((1,), (1,)), ((), ())), preferred_element_type=F32)


def _proj_kernel(x_ref, gmix_ref, w1_ref, gq_ref, gkv_ref, wq_ref, wkv_ref,
                 cos_ref, sa_ref, sb_ref, vone_ref,
                 qm_ref, km_ref, vm_ref, dq_ref, dk_ref, dv_ref):
    x = x_ref[...]
    u = _rms(x, gmix_ref[...], EPS).astype(BF16)
    z = _dot(u, w1_ref[...])
    cq = _rms(z[:, C_CQ:C_CKV], gq_ref[...], EPS).astype(BF16)
    ckv = _rms(z[:, C_CKV:C_KR], gkv_ref[...], EPS).astype(BF16)
    q = _dot(cq, wq_ref[...])
    kv = _dot(ckv, wkv_ref[...])
    cos = cos_ref[...]
    sa = sa_ref[...]
    sb = sb_ref[...]

    def rope(t):
        return t * cos + pltpu.roll(t, LANES - 16, 1) * sa + pltpu.roll(t, 16, 1) * sb

    k_rope = rope(z[:, C_KR:C_DQ])
    q_scale = (MLA_NOPE + MLA_ROPE) ** -0.5 * LOG2E
    nk = MLA_HEADS * LANES
    v_all = kv[:, nk:] + vone_ref[...]
    for h in range(MLA_HEADS):
        sl = slice(h * LANES, (h + 1) * LANES)
        qm_ref[0, h] = (rope(q[:, sl]) * q_scale).astype(BF16)
        km_ref[0, h] = (kv[:, sl] + k_rope).astype(BF16)
        vm_ref[0, h] = v_all[:, sl].astype(BF16)
    d_scale = DIFF_D ** -0.5 * LOG2E
    for h in range(DIFF_HEADS):
        dq_ref[0, h] = (z[:, C_DQ + h * LANES:C_DQ + (h + 1) * LANES] * d_scale).astype(BF16)
        dk_ref[0, h] = z[:, C_DK + h * LANES:C_DK + (h + 1) * LANES].astype(BF16)
        dv_ref[0, h] = z[:, C_DV + h * LANES:C_DV + (h + 1) * LANES].astype(BF16)


def _const_spec(shape):
    nd = len(shape)
    return pl.BlockSpec(shape, lambda *_: (0,) * nd, pipeline_mode=pl.Buffered(1))


def _proj_call(xf, gmix, w1, gq, gkv, wq, wkv, cos_t, sa_t, sb_t, vone, batch, seq):
    tm = TM_PROJ
    nsb = seq // tm
    tokens = batch * seq
    head_spec = lambda nh: pl.BlockSpec((1, nh, tm, LANES), lambda i: (i // nsb, 0, i % nsb, 0))
    tab_spec = pl.BlockSpec((tm, LANES), lambda i: (i % nsb, 0))
    mshape = jax.ShapeDtypeStruct((batch, MLA_HEADS, seq, LANES), BF16)
    dshape = jax.ShapeDtypeStruct((batch, DIFF_HEADS, seq, LANES), BF16)
    return pl.pallas_call(
        _proj_kernel,
        grid=(tokens // tm,),
        in_specs=[
            pl.BlockSpec((tm, D_MODEL), lambda i: (i, 0)),
            _const_spec(gmix.shape), _const_spec(w1.shape), _const_spec(gq.shape),
            _const_spec(gkv.shape), _const_spec(wq.shape), _const_spec(wkv.shape),
            tab_spec, tab_spec, tab_spec, _const_spec(vone.shape),
        ],
        out_specs=[head_spec(MLA_HEADS)] * 3 + [head_spec(DIFF_HEADS)] * 3,
        out_shape=[mshape] * 3 + [dshape] * 3,
        compiler_params=pltpu.CompilerParams(
            dimension_semantics=("arbitrary",), vmem_limit_bytes=VMEM_LIMIT),
        name="proj",
    )(xf, gmix, w1, gq, gkv, wq, wkv, cos_t, sa_t, sb_t, vone)


def _mla_kernel(q_ref, k_ref, v_ref, o_ref, s_ref, m_ref, *, seq):
    tq = TQ_MLA
    nb = seq // tq
    nkt = seq // KEY_TILE
    lane = lax.broadcasted_iota(jnp.int32, (tq, LANES), 1)

    def scores(blk, slot):
        for hh in range(2):
            q = q_ref[0, hh, blk * tq:(blk + 1) * tq, :]
            macc = None
            for kt in range(nkt):
                cols = slice(kt * KEY_TILE, (kt + 1) * KEY_TILE)
                s = _dot_nt(q, k_ref[0, hh, cols, :])
                s_ref[slot, hh, :, cols] = s
                t = jnp.maximum(s[:, :LANES], s[:, LANES:])
                macc = t if macc is None else jnp.maximum(macc, t)
            m = jnp.max(macc, axis=-1, keepdims=True)
            m_ref[slot, hh] = jnp.broadcast_to(m, (tq, LANES))

    def values(blk, slot):
        acc = None
        for kt in range(nkt):
            cols = slice(kt * KEY_TILE, (kt + 1) * KEY_TILE)
            ps = []
            for hh in range(2):
                mb = m_ref[slot, hh]
                p = jnp.exp2(s_ref[slot, hh, :, cols] - jnp.concatenate([mb, mb], axis=1))
                ps.append(p.astype(BF16))
            vt = jnp.concatenate([v_ref[0, 0, cols, :], v_ref[0, 1, cols, :]], axis=1)
            part = _dot(jnp.concatenate(ps, axis=0), vt)
            acc = part if acc is None else acc + part
        outs = []
        for hh in range(2):
            a = acc[hh * tq:(hh + 1) * tq, hh * LANES:(hh + 1) * LANES]
            lcol = MLA_V if hh == 0 else 0
            outs.append(a * (1.0 / a[:, lcol:lcol + 1]))
        o_ref[0, blk * tq:(blk + 1) * tq, :] = (
            jnp.where(lane < MLA_V, outs[0], outs[1]).astype(o_ref.dtype))

    scores(0, 0)
    for blk in range(nb):
        if blk + 1 < nb:
            scores(blk + 1, (blk + 1) % 2)
        values(blk, blk % 2)


def _mla_call(qm, km, vm):
    batch, _, seq, _ = qm.shape
    in_spec = pl.BlockSpec((1, 2, seq, LANES), lambda b, g: (b, g, 0, 0))
    return pl.pallas_call(
        functools.partial(_mla_kernel, seq=seq),
        grid=(batch, MLA_HEADS // 2),
        in_specs=[in_spec, in_spec, in_spec],
        out_specs=pl.BlockSpec((1, seq, LANES), lambda b, g: (b, 0, g)),
        out_shape=jax.ShapeDtypeStruct((batch, seq, MLA_HEADS * MLA_V), BF16),
        scratch_shapes=[pltpu.VMEM((2, 2, TQ_MLA, seq), F32),
                        pltpu.VMEM((2, 2, TQ_MLA, LANES), F32)],
        compiler_params=pltpu.CompilerParams(
            dimension_semantics=("arbitrary", "arbitrary"), vmem_limit_bytes=VMEM_LIMIT),
        name="mla_attn",
    )(qm, km, vm)


def _diff_kernel(lam_ref, gsub_ref, roles_ref, q_ref, k_ref, v_ref, o_ref,
                 bias_ref, kfeat_ref, k12_ref, vaug_ref, s_ref, m_ref, *, seq, lam_init):
    tq = TQ_DIFF
    b = pl.program_id(0)
    h = pl.program_id(1)
    nb = seq // tq
    nkt = seq // KEY_TILE
    assert tq == KEY_TILE

    hv = jnp.full((1, 1), h, jnp.int32)
    slope = jnp.zeros((1, 1), F32)
    for hh in range(DIFF_HEADS):
        slope = jnp.where(hv == hh, 2.0 ** (-8.0 * (hh + 1.0) / DIFF_HEADS), slope)

    @pl.when(b == 0)
    def _():
        r = lax.broadcasted_iota(jnp.int32, (tq, KEY_TILE), 0)
        c = lax.broadcasted_iota(jnp.int32, (tq, KEY_TILE), 1)
        bias_ref[h] = jnp.abs(r - c).astype(F32) * (-LOG2E * slope)
        j = lax.broadcasted_iota(jnp.int32, (seq, LANES), 0)
        j_tile = (j - (j & (KEY_TILE - 1))).astype(F32)
        j_in = (j & (KEY_TILE - 1)).astype(F32)
        for mi in range(2):
            kf = (slope * roles_ref[mi, 0:1] + roles_ref[mi, 1:2] * j_tile
                  + roles_ref[mi, 2:3] * j_in)
            kfeat_ref[h, mi] = kf.astype(BF16)

    lp = lam_ref[...]
    lam = (jnp.exp(jnp.sum(lp[0:1] * lp[1:2], axis=-1, keepdims=True))
           - jnp.exp(jnp.sum(lp[2:3] * lp[3:4], axis=-1, keepdims=True)) + lam_init)

    k = k_ref[0, 0]
    lane_k = lax.broadcasted_iota(jnp.int32, k.shape, 1)
    k12_ref[0] = jnp.where(lane_k < DIFF_D, k, kfeat_ref[h, 0])
    k12_ref[1] = jnp.where(lane_k >= DIFF_D, k, kfeat_ref[h, 1])
    vaug_ref[:, :LANES] = v_ref[0, 0]
    vaug_ref[:, LANES:] = (lane_k == 0).astype(BF16)
    g = gsub_ref[...] * (1.0 - lam_init)
    lane_q = lax.broadcasted_iota(jnp.int32, (tq, LANES), 1)
    i_in = lax.broadcasted_iota(jnp.int32, (tq, LANES), 0).astype(F32)

    def scores(blk, slot):
        q = q_ref[0, 0, blk * tq:(blk + 1) * tq, :]
        for mi in range(2):
            data = (lane_q < DIFF_D) if mi == 0 else (lane_q >= DIFF_D)
            qf = (roles_ref[mi, 3:4] * float(blk * tq) + roles_ref[mi, 4:5] * i_in
                  + slope * roles_ref[mi, 5:6])
            lhs = {0: jnp.where(data, q, jnp.zeros_like(q))}
            if blk > 0:
                lhs[1] = jnp.where(data, q, qf.astype(BF16))
            if blk < nb - 1:
                lhs[-1] = jnp.where(data, q, (-qf).astype(BF16))
            macc = None
            for kt in range(nkt):
                cols = slice(kt * KEY_TILE, (kt + 1) * KEY_TILE)
                s = _dot_nt(lhs[(blk > kt) - (blk < kt)], k12_ref[mi, cols, :])
                if kt == blk:
                    s = s + bias_ref[h]
                s_ref[slot, mi, :, cols] = s
                t = jnp.maximum(s[:, :LANES], s[:, LANES:])
                macc = t if macc is None else jnp.maximum(macc, t)
            m = jnp.max(macc, axis=-1, keepdims=True)
            m_ref[slot, mi] = jnp.broadcast_to(m, (tq, LANES))

    def values(blk, slot):
        outs = []
        for mi in range(2):
            acc = None
            for kt in range(nkt):
                cols = slice(kt * KEY_TILE, (kt + 1) * KEY_TILE)
                mb = m_ref[slot, mi]
                p = jnp.exp2(s_ref[slot, mi, :, cols] - jnp.concatenate([mb, mb], axis=1))
                part = _dot(p.astype(BF16), vaug_ref[cols, :])
                acc = part if acc is None else acc + part
            outs.append(acc[:, :LANES] * (1.0 / acc[:, LANES:LANES + 1]))
        o = outs[0] - lam * outs[1]
        o_ref[0, blk * tq:(blk + 1) * tq, :] = _rms(o, g, SUBLN_EPS).astype(o_ref.dtype)

    scores(0, 0)
    for blk in range(nb):
        if blk + 1 < nb:
            scores(blk + 1, (blk + 1) % 2)
        values(blk, blk % 2)


def _alibi_roles():
    terms = []
    rest = LOG2E
    for _ in range(3):
        t = float(np.float32(rest).astype(ml_dtypes.bfloat16))
        terms.append(t)
        rest -= t
    roles = np.zeros((2, 6, LANES), np.float32)
    for mi in range(2):
        base = DIFF_D if mi == 0 else 0
        for t, lt in enumerate(terms):
            roles[mi, 0, base + t] = -lt
            roles[mi, 0, base + 3 + t] = -lt
            roles[mi, 1, base + 6 + t] = 1.0
            roles[mi, 2, base + 9 + t] = 1.0
            roles[mi, 3, base + t] = 1.0
            roles[mi, 4, base + 3 + t] = 1.0
            roles[mi, 5, base + 6 + t] = lt
            roles[mi, 5, base + 9 + t] = lt
    return jnp.asarray(roles)


def _diff_call(lam_p, gsub, dq, dk, dv, lam_init):
    batch, _, seq, _ = dq.shape
    roles = _alibi_roles()
    in_spec = pl.BlockSpec((1, 1, seq, LANES), lambda b, h: (b, h, 0, 0))
    return pl.pallas_call(
        functools.partial(_diff_kernel, seq=seq, lam_init=lam_init),
        grid=(batch, DIFF_HEADS),
        in_specs=[_const_spec(lam_p.shape), _const_spec(gsub.shape), _const_spec(roles.shape),
                  in_spec, in_spec, in_spec],
        out_specs=pl.BlockSpec((1, seq, LANES), lambda b, h: (b, 0, h)),
        out_shape=jax.ShapeDtypeStruct((batch, seq, DIFF_HEADS * LANES), BF16),
        scratch_shapes=[pltpu.VMEM((DIFF_HEADS, TQ_DIFF, KEY_TILE), F32),
                        pltpu.VMEM((DIFF_HEADS, 2, seq, LANES), BF16),
                        pltpu.VMEM((2, seq, LANES), BF16),
                        pltpu.VMEM((seq, 2 * LANES), BF16),
                        pltpu.VMEM((2, 2, TQ_DIFF, seq), F32),
                        pltpu.VMEM((2, 2, TQ_DIFF, LANES), F32)],
        compiler_params=pltpu.CompilerParams(
            dimension_semantics=("arbitrary", "arbitrary"), vmem_limit_bytes=VMEM_LIMIT),
        name="diff_attn",
    )(lam_p, gsub, roles, dq, dk, dv)


def _mlp_kernel(om_ref, od_ref, h_ref, wo1_ref, wo2_ref, gmlp_ref, wup_ref, wdn_ref, gfin_ref,
                out_ref, *, final):
    a = h_ref[...] + _dot(om_ref[...], wo1_ref[...]) + _dot(od_ref[...], wo2_ref[...])
    u = _rms(a, gmlp_ref[...], EPS).astype(BF16)
    out_ref[...] = a
    for c in range(D_FF // FF_CHUNK):
        f = jnp.maximum(_dot(u, wup_ref[:, c * FF_CHUNK:(c + 1) * FF_CHUNK]), 0.0)
        out_ref[...] += _dot((f * f).astype(BF16), wdn_ref[c * FF_CHUNK:(c + 1) * FF_CHUNK, :])
    if final:
        out_ref[...] = _rms(out_ref[...], gfin_ref[...], EPS)


def _mlp_call(om, od, hf, wo1, wo2, gmlp, wup, wdn, gfin, final):
    tm = TM_MLP
    tokens = hf.shape[0]
    row = lambda w: pl.BlockSpec((tm, w), lambda i: (i, 0))
    return pl.pallas_call(
        functools.partial(_mlp_kernel, final=final),
        grid=(tokens // tm,),
        in_specs=[row(om.shape[1]), row(od.shape[1]), row(D_MODEL),
                  _const_spec(wo1.shape), _const_spec(wo2.shape), _const_spec(gmlp.shape),
                  _const_spec(wup.shape), _const_spec(wdn.shape), _const_spec(gfin.shape)],
        out_specs=row(D_MODEL),
        out_shape=jax.ShapeDtypeStruct(hf.shape, F32),
        compiler_params=pltpu.CompilerParams(
            dimension_semantics=("arbitrary",), vmem_limit_bytes=VMEM_LIMIT),
        name="outproj_mlp",
    )(om, od, hf, wo1, wo2, gmlp, wup, wdn, gfin)


def _pack_w_in(w):
    zeros = lambda n: jnp.zeros((D_MODEL, n), w.dtype)
    off_kr = Q_LORA + KV_LORA
    off_dq = off_kr + MLA_ROPE
    kr = jnp.concatenate([zeros(MLA_NOPE), w[:, off_kr:off_dq], zeros(LANES - MLA_NOPE - MLA_ROPE)], 1)
    return jnp.concatenate([w[:, :off_kr], kr, w[:, off_dq:]], axis=1).astype(BF16)


def _pack_w_uq(w):
    hd = MLA_NOPE + MLA_ROPE
    w = w.reshape(Q_LORA, MLA_HEADS, hd)
    w = jnp.pad(w, ((0, 0), (0, 0), (0, LANES - hd)))
    return w.reshape(Q_LORA, MLA_HEADS * LANES).astype(BF16)


def _pack_w_ukv(w):
    w = w.reshape(KV_LORA, MLA_HEADS, MLA_NOPE + MLA_V)
    wk = jnp.pad(w[:, :, :MLA_NOPE], ((0, 0), (0, 0), (0, LANES - MLA_NOPE)))
    wv = w[:, :, MLA_NOPE:]
    pad = jnp.zeros_like(wv)
    wv_even = jnp.concatenate([wv, pad], axis=-1)
    wv_odd = jnp.concatenate([pad, wv], axis=-1)
    odd = (jnp.arange(MLA_HEADS) % 2 == 1)[None, :, None]
    wv = jnp.where(odd, wv_odd, wv_even)
    return jnp.concatenate([wk.reshape(KV_LORA, -1), wv.reshape(KV_LORA, -1)], axis=1).astype(BF16)


def _rope_tables(seq):
    pos = np.arange(seq, dtype=np.float32)
    inv_freq = (1.0 / (ROPE_THETA ** (np.arange(0, MLA_ROPE, 2, dtype=np.float32) / MLA_ROPE))).astype(np.float32)
    ang = jnp.asarray(pos[:, None] * inv_freq[None, :])
    cos, sin = jnp.cos(ang), jnp.sin(ang)
    half = MLA_ROPE // 2
    ones = jnp.ones((seq, MLA_NOPE), F32)
    z = lambda n: jnp.zeros((seq, n), F32)
    tail = LANES - MLA_NOPE - MLA_ROPE
    cos_t = jnp.concatenate([ones, cos, cos, z(tail)], axis=1)
    sa_t = jnp.concatenate([z(MLA_NOPE), -sin, z(half), z(tail)], axis=1)
    sb_t = jnp.concatenate([z(MLA_NOPE), z(half), sin, z(tail)], axis=1)
    return cos_t, sa_t, sb_t


def _v_ones():
    v = np.zeros((1, MLA_HEADS * LANES), np.float32)
    for h in range(MLA_HEADS):
        v[0, h * LANES + (MLA_V if h % 2 == 0 else 0)] = 1.0
    return jnp.asarray(v)


def kernel(x, w_in, g_mix, g_q, g_kv, w_uq, w_ukv, lam_q1, lam_k1, lam_q2, lam_k2,
           g_sub, w_out, g_mlp, w_up, w_down, g_final):
    batch, seq, _ = x.shape
    depth = w_in.shape[0]
    cos_t, sa_t, sb_t = _rope_tables(seq)
    vone = _v_ones()
    hf = x.reshape(batch * seq, D_MODEL)
    for l in range(depth):
        lam_init = 0.8 - 0.6 * math.exp(-0.3 * l)
        qm, km, vm, dq, dk, dv = _proj_call(
            hf, g_mix[l][None], _pack_w_in(w_in[l]), g_q[l][None], g_kv[l][None],
            _pack_w_uq(w_uq[l]), _pack_w_ukv(w_ukv[l]), cos_t, sa_t, sb_t, vone, batch, seq)
        o_mla = _mla_call(qm, km, vm)
        lam_p = jnp.stack([lam_q1[l], lam_k1[l], lam_q2[l], lam_k2[l]]).astype(F32)
        o_diff = _diff_call(lam_p, g_sub[l][None], dq, dk, dv, lam_init)
        wo = w_out[l].astype(BF16)
        n_mla = MLA_HEADS * MLA_V
        hf = _mlp_call(o_mla.reshape(batch * seq, -1), o_diff.reshape(batch * seq, -1), hf,
                       wo[:n_mla], wo[n_mla:], g_mlp[l][None], w_up[l].astype(BF16),
                       w_down[l].astype(BF16), g_final[None], final=(l == depth - 1))
    return hf.reshape(batch, seq, D_MODEL)
```

```python
import functools
import math

import ml_dtypes
import numpy as np
import jax
import jax.numpy as jnp
from jax import lax
from jax.experimental import pallas as pl
from jax.experimental.pallas import tpu as pltpu

D_MODEL = 1024
MLA_HEADS = 8
MLA_NOPE = 64
MLA_ROPE = 32
MLA_V = 64
Q_LORA = 256
KV_LORA = 256
DIFF_HEADS = 4
DIFF_D = 64
D_FF = 4 * D_MODEL
ROPE_THETA = 10000.0
EPS = 1e-6
SUBLN_EPS = 1e-5

LANES = 128
LOG2E = math.log2(math.e)

C_CQ = 0
C_CKV = C_CQ + Q_LORA
C_KR = C_CKV + KV_LORA
C_DQ = C_KR + LANES
C_DK = C_DQ + DIFF_HEADS * LANES
C_DV = C_DK + DIFF_HEADS * LANES
P_PACK = C_DV + DIFF_HEADS * LANES

TM_PROJ = 512
TM_MLP = 1024
TQ_MLA = 256
TQ_DIFF = 256
FF_CHUNK = 1024
KEY_TILE = 256
VMEM_LIMIT = 56 * 1024 * 1024

F32 = jnp.float32
BF16 = jnp.bfloat16


def _rms(x, g, eps):
    return x * lax.rsqrt(jnp.mean(x * x, axis=-1, keepdims=True) + eps) * g


def _dot(a, b):
    return jnp.dot(a, b, preferred_element_type=F32)


def _dot_nt(a, b):
    return lax.dot_general(a, b, (((1,), (1,)), ((), ())), preferred_element_type=F32)


def _proj_kernel(x_ref, gmix_ref, w1_ref, gq_ref, gkv_ref, wq_ref, wkv_ref,
                 cos_ref, sa_ref, sb_ref, vone_ref,
                 qm_ref, km_ref, vm_ref, dq_ref, dk_ref, dv_ref):
    x = x_ref[...]
    u = _rms(x, gmix_ref[...], EPS).astype(BF16)
    z = _dot(u, w1_ref[...])
    cq = _rms(z[:, C_CQ:C_CKV], gq_ref[...], EPS).astype(BF16)
    ckv = _rms(z[:, C_CKV:C_KR], gkv_ref[...], EPS).astype(BF16)
    q = _dot(cq, wq_ref[...])
    kv = _dot(ckv, wkv_ref[...])
    cos = cos_ref[...]
    sa = sa_ref[...]
    sb = sb_ref[...]

    def rope(t):
        return t * cos + pltpu.roll(t, LANES - 16, 1) * sa + pltpu.roll(t, 16, 1) * sb

    k_rope = rope(z[:, C_KR:C_DQ])
    q_scale = (MLA_NOPE + MLA_ROPE) ** -0.5 * LOG2E
    nk = MLA_HEADS * LANES
    v_all = kv[:, nk:] + vone_ref[...]
    for h in range(MLA_HEADS):
        sl = slice(h * LANES, (h + 1) * LANES)
        qm_ref[0, h] = (rope(q[:, sl]) * q_scale).astype(BF16)
        km_ref[0, h] = (kv[:, sl] + k_rope).astype(BF16)
        vm_ref[0, h] = v_all[:, sl].astype(BF16)
    d_scale = DIFF_D ** -0.5 * LOG2E
    for h in range(DIFF_HEADS):
        dq_ref[0, h] = (z[:, C_DQ + h * LANES:C_DQ + (h + 1) * LANES] * d_scale).astype(BF16)
        dk_ref[0, h] = z[:, C_DK + h * LANES:C_DK + (h + 1) * LANES].astype(BF16)
        dv_ref[0, h] = z[:, C_DV + h * LANES:C_DV + (h + 1) * LANES].astype(BF16)


def _const_spec(shape):
    nd = len(shape)
    return pl.BlockSpec(shape, lambda *_: (0,) * nd, pipeline_mode=pl.Buffered(1))


def _proj_call(xf, gmix, w1, gq, gkv, wq, wkv, cos_t, sa_t, sb_t, vone, batch, seq):
    tm = TM_PROJ
    nsb = seq // tm
    tokens = batch * seq
    head_spec = lambda nh: pl.BlockSpec((1, nh, tm, LANES), lambda i: (i // nsb, 0, i % nsb, 0))
    tab_spec = pl.BlockSpec((tm, LANES), lambda i: (i % nsb, 0))
    mshape = jax.ShapeDtypeStruct((batch, MLA_HEADS, seq, LANES), BF16)
    dshape = jax.ShapeDtypeStruct((batch, DIFF_HEADS, seq, LANES), BF16)
    return pl.pallas_call(
        _proj_kernel,
        grid=(tokens // tm,),
        in_specs=[
            pl.BlockSpec((tm, D_MODEL), lambda i: (i, 0)),
            _const_spec(gmix.shape), _const_spec(w1.shape), _const_spec(gq.shape),
            _const_spec(gkv.shape), _const_spec(wq.shape), _const_spec(wkv.shape),
            tab_spec, tab_spec, tab_spec, _const_spec(vone.shape),
        ],
        out_specs=[head_spec(MLA_HEADS)] * 3 + [head_spec(DIFF_HEADS)] * 3,
        out_shape=[mshape] * 3 + [dshape] * 3,
        compiler_params=pltpu.CompilerParams(
            dimension_semantics=("arbitrary",), vmem_limit_bytes=VMEM_LIMIT),
        name="proj",
    )(xf, gmix, w1, gq, gkv, wq, wkv, cos_t, sa_t, sb_t, vone)


def _mla_kernel(q_ref, k_ref, v_ref, o_ref, s_ref, m_ref, *, seq):
    tq = TQ_MLA
    nb = seq // tq
    nkt = seq // KEY_TILE
    lane = lax.broadcasted_iota(jnp.int32, (tq, LANES), 1)

    def scores(blk, slot):
        for hh in range(2):
            q = q_ref[0, hh, blk * tq:(blk + 1) * tq, :]
            macc = None
            for kt in range(nkt):
                cols = slice(kt * KEY_TILE, (kt + 1) * KEY_TILE)
                s = _dot_nt(q, k_ref[0, hh, cols, :])
                s_ref[slot, hh, :, cols] = s
                t = jnp.maximum(s[:, :LANES], s[:, LANES:])
                macc = t if macc is None else jnp.maximum(macc, t)
            m = jnp.max(macc, axis=-1, keepdims=True)
            m_ref[slot, hh] = jnp.broadcast_to(m, (tq, LANES))

    def values(blk, slot):
        acc = None
        for kt in range(nkt):
            cols = slice(kt * KEY_TILE, (kt + 1) * KEY_TILE)
            ps = []
            for hh in range(2):
                mb = m_ref[slot, hh]
                p = jnp.exp2(s_ref[slot, hh, :, cols] - jnp.concatenate([mb, mb], axis=1))
                ps.append(p.astype(BF16))
            vt = jnp.concatenate([v_ref[0, 0, cols, :], v_ref[0, 1, cols, :]], axis=1)
            part = _dot(jnp.concatenate(ps, axis=0), vt)
            acc = part if acc is None else acc + part
        outs = []
        for hh in range(2):
            a = acc[hh * tq:(hh + 1) * tq, hh * LANES:(hh + 1) * LANES]
            lcol = MLA_V if hh == 0 else 0
            outs.append(a * (1.0 / a[:, lcol:lcol + 1]))
        o_ref[0, blk * tq:(blk + 1) * tq, :] = (
            jnp.where(lane < MLA_V, outs[0], outs[1]).astype(o_ref.dtype))

    scores(0, 0)
    for blk in range(nb):
        if blk + 1 < nb:
            scores(blk + 1, (blk + 1) % 2)
        values(blk, blk % 2)


def _mla_call(qm, km, vm):
    batch, _, seq, _ = qm.shape
    in_spec = pl.BlockSpec((1, 2, seq, LANES), lambda b, g: (b, g, 0, 0))
    return pl.pallas_call(
        functools.partial(_mla_kernel, seq=seq),
        grid=(batch, MLA_HEADS // 2),
        in_specs=[in_spec, in_spec, in_spec],
        out_specs=pl.BlockSpec((1, seq, LANES), lambda b, g: (b, 0, g)),
        out_shape=jax.ShapeDtypeStruct((batch, seq, MLA_HEADS * MLA_V), BF16),
        scratch_shapes=[pltpu.VMEM((2, 2, TQ_MLA, seq), F32),
                        pltpu.VMEM((2, 2, TQ_MLA, LANES), F32)],
        compiler_params=pltpu.CompilerParams(
            dimension_semantics=("arbitrary", "arbitrary"), vmem_limit_bytes=VMEM_LIMIT),
        name="mla_attn",
    )(qm, km, vm)


def _diff_kernel(lam_ref, gsub_ref, roles_ref, q_ref, k_ref, v_ref, o_ref,
                 bias_ref, kfeat_ref, k12_ref, vaug_ref, s_ref, m_ref, *, seq, lam_init):
    tq = TQ_DIFF
    b = pl.program_id(0)
    h = pl.program_id(1)
    nb = seq // tq
    nkt = seq // KEY_TILE
    assert tq == KEY_TILE

    hv = jnp.full((1, 1), h, jnp.int32)
    slope = jnp.zeros((1, 1), F32)
    for hh in range(DIFF_HEADS):
        slope = jnp.where(hv == hh, 2.0 ** (-8.0 * (hh + 1.0) / DIFF_HEADS), slope)

    @pl.when(b == 0)
    def _():
        r = lax.broadcasted_iota(jnp.int32, (tq, KEY_TILE), 0)
        c = lax.broadcasted_iota(jnp.int32, (tq, KEY_TILE), 1)
        bias_ref[h] = jnp.abs(r - c).astype(F32) * (-LOG2E * slope)
        j = lax.broadcasted_iota(jnp.int32, (seq, LANES), 0)
        j_tile = (j - (j & (KEY_TILE - 1))).astype(F32)
        j_in = (j & (KEY_TILE - 1)).astype(F32)
        for mi in range(2):
            kf = (slope * roles_ref[mi, 0:1] + roles_ref[mi, 1:2] * j_tile
                  + roles_ref[mi, 2:3] * j_in)
            kfeat_ref[h, mi] = kf.astype(BF16)

    lp = lam_ref[...]
    lam = (jnp.exp(jnp.sum(lp[0:1] * lp[1:2], axis=-1, keepdims=True))
           - jnp.exp(jnp.sum(lp[2:3] * lp[3:4], axis=-1, keepdims=True)) + lam_init)

    k = k_ref[0, 0]
    lane_k = lax.broadcasted_iota(jnp.int32, k.shape, 1)
    k12_ref[0] = jnp.where(lane_k < DIFF_D, k, kfeat_ref[h, 0])
    k12_ref[1] = jnp.where(lane_k >= DIFF_D, k, kfeat_ref[h, 1])
    vaug_ref[:, :LANES] = v_ref[0, 0]
    vaug_ref[:, LANES:] = (lane_k == 0).astype(BF16)
    g = gsub_ref[...] * (1.0 - lam_init)
    lane_q = lax.broadcasted_iota(jnp.int32, (tq, LANES), 1)
    i_in = lax.broadcasted_iota(jnp.int32, (tq, LANES), 0).astype(F32)

    def scores(blk, slot):
        q = q_ref[0, 0, blk * tq:(blk + 1) * tq, :]
        for mi in range(2):
            data = (lane_q < DIFF_D) if mi == 0 else (lane_q >= DIFF_D)
            qf = (roles_ref[mi, 3:4] * float(blk * tq) + roles_ref[mi, 4:5] * i_in
                  + slope * roles_ref[mi, 5:6])
            lhs = {0: jnp.where(data, q, jnp.zeros_like(q))}
            if blk > 0:
                lhs[1] = jnp.where(data, q, qf.astype(BF16))
            if blk < nb - 1:
                lhs[-1] = jnp.where(data, q, (-qf).astype(BF16))
            macc = None
            for kt in range(nkt):
                cols = slice(kt * KEY_TILE, (kt + 1) * KEY_TILE)
                s = _dot_nt(lhs[(blk > kt) - (blk < kt)], k12_ref[mi, cols, :])
                if kt == blk:
                    s = s + bias_ref[h]
                s_ref[slot, mi, :, cols] = s
                t = jnp.maximum(s[:, :LANES], s[:, LANES:])
                macc = t if macc is None else jnp.maximum(macc, t)
            m = jnp.max(macc, axis=-1, keepdims=True)
            m_ref[slot, mi] = jnp.broadcast_to(m, (tq, LANES))

    def values(blk, slot):
        outs = []
        for mi in range(2):
            acc = None
            for kt in range(nkt):
                cols = slice(kt * KEY_TILE, (kt + 1) * KEY_TILE)
                mb = m_ref[slot, mi]
                p = jnp.exp2(s_ref[slot, mi, :, cols] - jnp.concatenate([mb, mb], axis=1))
                part = _dot(p.astype(BF16), vaug_ref[cols, :])
                acc = part if acc is None else acc + part
            outs.append(acc[:, :LANES] * (1.0 / acc[:, LANES:LANES + 1]))
        o = outs[0] - lam * outs[1]
        o_ref[0, blk * tq:(blk + 1) * tq, :] = _rms(o, g, SUBLN_EPS).astype(o_ref.dtype)

    scores(0, 0)
    for blk in range(nb):
        if blk + 1 < nb:
            scores(blk + 1, (blk + 1) % 2)
        values(blk, blk % 2)


def _alibi_roles():
    terms = []
    rest = LOG2E
    for _ in range(3):
        t = float(np.float32(rest).astype(ml_dtypes.bfloat16))
        terms.append(t)
        rest -= t
    roles = np.zeros((2, 6, LANES), np.float32)
    for mi in range(2):
        base = DIFF_D if mi == 0 else 0
        for t, lt in enumerate(terms):
            roles[mi, 0, base + t] = -lt
            roles[mi, 0, base + 3 + t] = -lt
            roles[mi, 1, base + 6 + t] = 1.0
            roles[mi, 2, base + 9 + t] = 1.0
            roles[mi, 3, base + t] = 1.0
            roles[mi, 4, base + 3 + t] = 1.0
            roles[mi, 5, base + 6 + t] = lt
            roles[mi, 5, base + 9 + t] = lt
    return jnp.asarray(roles)


def _diff_call(lam_p, gsub, dq, dk, dv, lam_init):
    batch, _, seq, _ = dq.shape
    roles = _alibi_roles()
    in_spec = pl.BlockSpec((1, 1, seq, LANES), lambda b, h: (b, h, 0, 0))
    return pl.pallas_call(
        functools.partial(_diff_kernel, seq=seq, lam_init=lam_init),
        grid=(batch, DIFF_HEADS),
        in_specs=[_const_spec(lam_p.shape), _const_spec(gsub.shape), _const_spec(roles.shape),
                  in_spec, in_spec, in_spec],
        out_specs=pl.BlockSpec((1, seq, LANES), lambda b, h: (b, 0, h)),
        out_shape=jax.ShapeDtypeStruct((batch, seq, DIFF_HEADS * LANES), BF16),
        scratch_shapes=[pltpu.VMEM((DIFF_HEADS, TQ_DIFF, KEY_TILE), F32),
                        pltpu.VMEM((DIFF_HEADS, 2, seq, LANES), BF16),
                        pltpu.VMEM((2, seq, LANES), BF16),
                        pltpu.VMEM((seq, 2 * LANES), BF16),
                        pltpu.VMEM((2, 2, TQ_DIFF, seq), F32),
                        pltpu.VMEM((2, 2, TQ_DIFF, LANES), F32)],
        compiler_params=pltpu.CompilerParams(
            dimension_semantics=("arbitrary", "arbitrary"), vmem_limit_bytes=VMEM_LIMIT),
        name="diff_attn",
    )(lam_p, gsub, roles, dq, dk, dv)


def _mlp_kernel(om_ref, od_ref, h_ref, wo1_ref, wo2_ref, gmlp_ref, wup_ref, wdn_ref, gfin_ref,
                out_ref, *, final):
    a = h_ref[...] + _dot(om_ref[...], wo1_ref[...]) + _dot(od_ref[...], wo2_ref[...])
    u = _rms(a, gmlp_ref[...], EPS).astype(BF16)
    out_ref[...] = a
    for c in range(D_FF // FF_CHUNK):
        f = jnp.maximum(_dot(u, wup_ref[:, c * FF_CHUNK:(c + 1) * FF_CHUNK]), 0.0)
        out_ref[...] += _dot((f * f).astype(BF16), wdn_ref[c * FF_CHUNK:(c + 1) * FF_CHUNK, :])
    if final:
        out_ref[...] = _rms(out_ref[...], gfin_ref[...], EPS)


def _mlp_call(om, od, hf, wo1, wo2, gmlp, wup, wdn, gfin, final):
    tm = TM_MLP
    tokens = hf.shape[0]
    row = lambda w: pl.BlockSpec((tm, w), lambda i: (i, 0))
    return pl.pallas_call(
        functools.partial(_mlp_kernel, final=final),
        grid=(tokens // tm,),
        in_specs=[row(om.shape[1]), row(od.shape[1]), row(D_MODEL),
                  _const_spec(wo1.shape), _const_spec(wo2.shape), _const_spec(gmlp.shape),
                  _const_spec(wup.shape), _const_spec(wdn.shape), _const_spec(gfin.shape)],
        out_specs=row(D_MODEL),
        out_shape=jax.ShapeDtypeStruct(hf.shape, F32),
        compiler_params=pltpu.CompilerParams(
            dimension_semantics=("arbitrary",), vmem_limit_bytes=VMEM_LIMIT),
        name="outproj_mlp",
    )(om, od, hf, wo1, wo2, gmlp, wup, wdn, gfin)


def _pack_w_in(w):
    zeros = lambda n: jnp.zeros((D_MODEL, n), w.dtype)
    off_kr = Q_LORA + KV_LORA
    off_dq = off_kr + MLA_ROPE
    kr = jnp.concatenate([zeros(MLA_NOPE), w[:, off_kr:off_dq], zeros(LANES - MLA_NOPE - MLA_ROPE)], 1)
    return jnp.concatenate([w[:, :off_kr], kr, w[:, off_dq:]], axis=1).astype(BF16)


def _pack_w_uq(w):
    hd = MLA_NOPE + MLA_ROPE
    w = w.reshape(Q_LORA, MLA_HEADS, hd)
    w = jnp.pad(w, ((0, 0), (0, 0), (0, LANES - hd)))
    return w.reshape(Q_LORA, MLA_HEADS * LANES).astype(BF16)


def _pack_w_ukv(w):
    w = w.reshape(KV_LORA, MLA_HEADS, MLA_NOPE + MLA_V)
    wk = jnp.pad(w[:, :, :MLA_NOPE], ((0, 0), (0, 0), (0, LANES - MLA_NOPE)))
    wv = w[:, :, MLA_NOPE:]
    pad = jnp.zeros_like(wv)
    wv_even = jnp.concatenate([wv, pad], axis=-1)
    wv_odd = jnp.concatenate([pad, wv], axis=-1)
    odd = (jnp.arange(MLA_HEADS) % 2 == 1)[None, :, None]
    wv = jnp.where(odd, wv_odd, wv_even)
    return jnp.concatenate([wk.reshape(KV_LORA, -1), wv.reshape(KV_LORA, -1)], axis=1).astype(BF16)


def _rope_tables(seq):
    pos = np.arange(seq, dtype=np.float32)
    inv_freq = (1.0 / (ROPE_THETA ** (np.arange(0, MLA_ROPE, 2, dtype=np.float32) / MLA_ROPE))).astype(np.float32)
    ang = jnp.asarray(pos[:, None] * inv_freq[None, :])
    cos, sin = jnp.cos(ang), jnp.sin(ang)
    half = MLA_ROPE // 2
    ones = jnp.ones((seq, MLA_NOPE), F32)
    z = lambda n: jnp.zeros((seq, n), F32)
    tail = LANES - MLA_NOPE - MLA_ROPE
    cos_t = jnp.concatenate([ones, cos, cos, z(tail)], axis=1)
    sa_t = jnp.concatenate([z(MLA_NOPE), -sin, z(half), z(tail)], axis=1)
    sb_t = jnp.concatenate([z(MLA_NOPE), z(half), sin, z(tail)], axis=1)
    return cos_t, sa_t, sb_t


def _v_ones():
    v = np.zeros((1, MLA_HEADS * LANES), np.float32)
    for h in range(MLA_HEADS):
        v[0, h * LANES + (MLA_V if h % 2 == 0 else 0)] = 1.0
    return jnp.asarray(v)


def kernel(x, w_in, g_mix, g_q, g_kv, w_uq, w_ukv, lam_q1, lam_k1, lam_q2, lam_k2,
           g_sub, w_out, g_mlp, w_up, w_down, g_final):
    batch, seq, _ = x.shape
    depth = w_in.shape[0]
    cos_t, sa_t, sb_t = _rope_tables(seq)
    vone = _v_ones()
    hf = x.reshape(batch * seq, D_MODEL)
    for l in range(depth):
        lam_init = 0.8 - 0.6 * math.exp(-0.3 * l)
        qm, km, vm, dq, dk, dv = _proj_call(
            hf, g_mix[l][None], _pack_w_in(w_in[l]), g_q[l][None], g_kv[l][None],
            _pack_w_uq(w_uq[l]), _pack_w_ukv(w_ukv[l]), cos_t, sa_t, sb_t, vone, batch, seq)
        o_mla = _mla_call(qm, km, vm)
        lam_p = jnp.stack([lam_q1[l], lam_k1[l], lam_q2[l], lam_k2[l]]).astype(F32)
        o_diff = _diff_call(lam_p, g_sub[l][None], dq, dk, dv, lam_init)
        wo = w_out[l].astype(BF16)
        n_mla = MLA_HEADS * MLA_V
        hf = _mlp_call(o_mla.reshape(batch * seq, -1), o_diff.reshape(batch * seq, -1), hf,
                       wo[:n_mla], wo[n_mla:], g_mlp[l][None], w_up[l].astype(BF16),
                       w_down[l].astype(BF16), g_final[None], final=(l == depth - 1))
    return hf.reshape(batch, seq, D_MODEL)
```

```python
import functools
import math

import ml_dtypes
import numpy as np
import jax
import jax.numpy as jnp
from jax import lax
from jax.experimental import pallas as pl
from jax.experimental.pallas import tpu as pltpu

D_MODEL = 1024
MLA_HEADS = 8
MLA_NOPE = 64
MLA_ROPE = 32
MLA_V = 64
Q_LORA = 256
KV_LORA = 256
DIFF_HEADS = 4
DIFF_D = 64
D_FF = 4 * D_MODEL
ROPE_THETA = 10000.0
EPS = 1e-6
SUBLN_EPS = 1e-5

LANES = 128
LOG2E = math.log2(math.e)

C_CQ = 0
C_CKV = C_CQ + Q_LORA
C_KR = C_CKV + KV_LORA
C_DQ = C_KR + LANES
C_DK = C_DQ + DIFF_HEADS * LANES
C_DV = C_DK + DIFF_HEADS * LANES
P_PACK = C_DV + DIFF_HEADS * LANES

TM_PROJ = 1024
TM_MLP = 1024
TQ_MLA = 256
TQ_DIFF = 256
FF_CHUNK = 1024
KEY_TILE = 256
VMEM_LIMIT = 56 * 1024 * 1024

F32 = jnp.float32
BF16 = jnp.bfloat16


def _rms(x, g, eps):
    return x * lax.rsqrt(jnp.mean(x * x, axis=-1, keepdims=True) + eps) * g


def _dot(a, b):
    return jnp.dot(a, b, preferred_element_type=F32)


def _dot_nt(a, b):
    return lax.dot_general(a, b, (((1,), (1,)), ((), ())), preferred_element_type=F32)


def _proj_kernel(x_ref, gmix_ref, w1_ref, gq_ref, gkv_ref, wq_ref, wkv_ref,
                 cos_ref, sa_ref, sb_ref, vone_ref,
                 qm_ref, km_ref, vm_ref, dq_ref, dk_ref, dv_ref):
    x = x_ref[...]
    u = _rms(x, gmix_ref[...], EPS).astype(BF16)
    z = _dot(u, w1_ref[...])
    cq = _rms(z[:, C_CQ:C_CKV], gq_ref[...], EPS).astype(BF16)
    ckv = _rms(z[:, C_CKV:C_KR], gkv_ref[...], EPS).astype(BF16)
    q = _dot(cq, wq_ref[...])
    kv = _dot(ckv, wkv_ref[...])
    cos = cos_ref[...]
    sa = sa_ref[...]
    sb = sb_ref[...]

    half = MLA_ROPE // 2

    def rope(t):
        return t * cos + pltpu.roll(t, LANES - half, 1) * sa + pltpu.roll(t, half, 1) * sb

    k_rope = rope(z[:, C_KR:C_DQ])
    q_scale = (MLA_NOPE + MLA_ROPE) ** -0.5 * LOG2E
    nk = MLA_HEADS * LANES
    v_all = kv[:, nk:] + vone_ref[...]
    for h in range(MLA_HEADS):
        sl = slice(h * LANES, (h + 1) * LANES)
        qm_ref[0, h] = (rope(q[:, sl]) * q_scale).astype(BF16)
        km_ref[0, h] = (kv[:, sl] + k_rope).astype(BF16)
        vm_ref[0, h] = v_all[:, sl].astype(BF16)
    d_scale = DIFF_D ** -0.5 * LOG2E
    for h in range(DIFF_HEADS):
        dq_ref[0, h] = (z[:, C_DQ + h * LANES:C_DQ + (h + 1) * LANES] * d_scale).astype(BF16)
        dk_ref[0, h] = z[:, C_DK + h * LANES:C_DK + (h + 1) * LANES].astype(BF16)
        dv_ref[0, h] = z[:, C_DV + h * LANES:C_DV + (h + 1) * LANES].astype(BF16)


def _const_spec(shape):
    nd = len(shape)
    return pl.BlockSpec(shape, lambda *_: (0,) * nd, pipeline_mode=pl.Buffered(1))


def _proj_call(xf, gmix, w1, gq, gkv, wq, wkv, cos_t, sa_t, sb_t, vone, batch, seq):
    tm = TM_PROJ
    nsb = seq // tm
    tokens = batch * seq
    head_spec = lambda nh: pl.BlockSpec((1, nh, tm, LANES), lambda i: (i // nsb, 0, i % nsb, 0))
    tab_spec = pl.BlockSpec((tm, LANES), lambda i: (i % nsb, 0))
    mshape = jax.ShapeDtypeStruct((batch, MLA_HEADS, seq, LANES), BF16)
    dshape = jax.ShapeDtypeStruct((batch, DIFF_HEADS, seq, LANES), BF16)
    return pl.pallas_call(
        _proj_kernel,
        grid=(tokens // tm,),
        in_specs=[
            pl.BlockSpec((tm, D_MODEL), lambda i: (i, 0)),
            _const_spec(gmix.shape), _const_spec(w1.shape), _const_spec(gq.shape),
            _const_spec(gkv.shape), _const_spec(wq.shape), _const_spec(wkv.shape),
            tab_spec, tab_spec, tab_spec, _const_spec(vone.shape),
        ],
        out_specs=[head_spec(MLA_HEADS)] * 3 + [head_spec(DIFF_HEADS)] * 3,
        out_shape=[mshape] * 3 + [dshape] * 3,
        compiler_params=pltpu.CompilerParams(
            dimension_semantics=("arbitrary",), vmem_limit_bytes=VMEM_LIMIT),
        name="proj",
    )(xf, gmix, w1, gq, gkv, wq, wkv, cos_t, sa_t, sb_t, vone)


def _mla_kernel(q_ref, k_ref, v_ref, o_ref, s_ref, m_ref, *, seq):
    tq = TQ_MLA
    nb = seq // tq
    nkt = seq // KEY_TILE
    lane = lax.broadcasted_iota(jnp.int32, (tq, LANES), 1)

    def scores(blk, slot):
        for hh in range(2):
            q = q_ref[0, hh, blk * tq:(blk + 1) * tq, :]
            macc = None
            for kt in range(nkt):
                cols = slice(kt * KEY_TILE, (kt + 1) * KEY_TILE)
                s = _dot_nt(q, k_ref[0, hh, cols, :])
                s_ref[slot, hh, :, cols] = s
                t = jnp.maximum(s[:, :LANES], s[:, LANES:])
                macc = t if macc is None else jnp.maximum(macc, t)
            m = jnp.max(macc, axis=-1, keepdims=True)
            m_ref[slot, hh] = jnp.broadcast_to(m, (tq, LANES))

    def values(blk, slot):
        acc = None
        for kt in range(nkt):
            cols = slice(kt * KEY_TILE, (kt + 1) * KEY_TILE)
            ps = []
            for hh in range(2):
                mb = m_ref[slot, hh]
                p = jnp.exp2(s_ref[slot, hh, :, cols] - jnp.concatenate([mb, mb], axis=1))
                ps.append(p.astype(BF16))
            vt = jnp.concatenate([v_ref[0, 0, cols, :], v_ref[0, 1, cols, :]], axis=1)
            part = _dot(jnp.concatenate(ps, axis=0), vt)
            acc = part if acc is None else acc + part
        outs = []
        for hh in range(2):
            a = acc[hh * tq:(hh + 1) * tq, hh * LANES:(hh + 1) * LANES]
            lcol = MLA_V if hh == 0 else 0
            outs.append(a * (1.0 / a[:, lcol:lcol + 1]))
        o_ref[0, blk * tq:(blk + 1) * tq, :] = (
            jnp.where(lane < MLA_V, outs[0], outs[1]).astype(o_ref.dtype))

    scores(0, 0)
    for blk in range(nb):
        if blk + 1 < nb:
            scores(blk + 1, (blk + 1) % 2)
        values(blk, blk % 2)


def _mla_call(qm, km, vm):
    batch, _, seq, _ = qm.shape
    in_spec = pl.BlockSpec((1, 2, seq, LANES), lambda b, g: (b, g, 0, 0))
    return pl.pallas_call(
        functools.partial(_mla_kernel, seq=seq),
        grid=(batch, MLA_HEADS // 2),
        in_specs=[in_spec, in_spec, in_spec],
        out_specs=pl.BlockSpec((1, seq, LANES), lambda b, g: (b, 0, g)),
        out_shape=jax.ShapeDtypeStruct((batch, seq, MLA_HEADS * MLA_V), BF16),
        scratch_shapes=[pltpu.VMEM((2, 2, TQ_MLA, seq), F32),
                        pltpu.VMEM((2, 2, TQ_MLA, LANES), F32)],
        compiler_params=pltpu.CompilerParams(
            dimension_semantics=("arbitrary", "arbitrary"), vmem_limit_bytes=VMEM_LIMIT),
        name="mla_attn",
    )(qm, km, vm)


def _diff_kernel(lam_ref, gsub_ref, roles_ref, q_ref, k_ref, v_ref, o_ref,
                 bias_ref, kfeat_ref, k12_ref, vaug_ref, s_ref, m_ref, *, seq, lam_init):
    tq = TQ_DIFF
    b = pl.program_id(0)
    h = pl.program_id(1)
    nb = seq // tq
    nkt = seq // KEY_TILE
    assert tq == KEY_TILE

    hv = jnp.full((1, 1), h, jnp.int32)
    slope = jnp.zeros((1, 1), F32)
    for hh in range(DIFF_HEADS):
        slope = jnp.where(hv == hh, 2.0 ** (-8.0 * (hh + 1.0) / DIFF_HEADS), slope)

    @pl.when(b == 0)
    def _():
        r = lax.broadcasted_iota(jnp.int32, (tq, KEY_TILE), 0)
        c = lax.broadcasted_iota(jnp.int32, (tq, KEY_TILE), 1)
        bias_ref[h] = jnp.abs(r - c).astype(F32) * (-LOG2E * slope)
        j = lax.broadcasted_iota(jnp.int32, (seq, LANES), 0)
        j_tile = (j - (j & (KEY_TILE - 1))).astype(F32)
        j_in = (j & (KEY_TILE - 1)).astype(F32)
        for mi in range(2):
            kf = (slope * roles_ref[mi, 0:1] + roles_ref[mi, 1:2] * j_tile
                  + roles_ref[mi, 2:3] * j_in)
            kfeat_ref[h, mi] = kf.astype(BF16)

    lp = lam_ref[...]
    lam = (jnp.exp(jnp.sum(lp[0:1] * lp[1:2], axis=-1, keepdims=True))
           - jnp.exp(jnp.sum(lp[2:3] * lp[3:4], axis=-1, keepdims=True)) + lam_init)

    k = k_ref[0, 0]
    lane_k = lax.broadcasted_iota(jnp.int32, k.shape, 1)
    k12_ref[0] = jnp.where(lane_k < DIFF_D, k, kfeat_ref[h, 0])
    k12_ref[1] = jnp.where(lane_k >= DIFF_D, k, kfeat_ref[h, 1])
    vaug_ref[:, :LANES] = v_ref[0, 0]
    vaug_ref[:, LANES:] = (lane_k == 0).astype(BF16)
    g = gsub_ref[...] * (1.0 - lam_init)
    lane_q = lax.broadcasted_iota(jnp.int32, (tq, LANES), 1)
    i_in = lax.broadcasted_iota(jnp.int32, (tq, LANES), 0).astype(F32)

    def scores(blk, slot):
        q = q_ref[0, 0, blk * tq:(blk + 1) * tq, :]
        for mi in range(2):
            data = (lane_q < DIFF_D) if mi == 0 else (lane_q >= DIFF_D)
            qf = (roles_ref[mi, 3:4] * float(blk * tq) + roles_ref[mi, 4:5] * i_in
                  + slope * roles_ref[mi, 5:6])
            lhs = {0: jnp.where(data, q, jnp.zeros_like(q))}
            if blk > 0:
                lhs[1] = jnp.where(data, q, qf.astype(BF16))
            if blk < nb - 1:
                lhs[-1] = jnp.where(data, q, (-qf).astype(BF16))
            macc = None
            for kt in range(nkt):
                cols = slice(kt * KEY_TILE, (kt + 1) * KEY_TILE)
                s = _dot_nt(lhs[(blk > kt) - (blk < kt)], k12_ref[mi, cols, :])
                if kt == blk:
                    s = s + bias_ref[h]
                s_ref[slot, mi, :, cols] = s
                t = jnp.maximum(s[:, :LANES], s[:, LANES:])
                macc = t if macc is None else jnp.maximum(macc, t)
            m = jnp.max(macc, axis=-1, keepdims=True)
            m_ref[slot, mi] = jnp.broadcast_to(m, (tq, LANES))

    def values(blk, slot):
        outs = []
        for mi in range(2):
            acc = None
            for kt in range(nkt):
                cols = slice(kt * KEY_TILE, (kt + 1) * KEY_TILE)
                mb = m_ref[slot, mi]
                p = jnp.exp2(s_ref[slot, mi, :, cols] - jnp.concatenate([mb, mb], axis=1))
                part = _dot(p.astype(BF16), vaug_ref[cols, :])
                acc = part if acc is None else acc + part
            outs.append(acc[:, :LANES] * (1.0 / acc[:, LANES:LANES + 1]))
        o = outs[0] - lam * outs[1]
        o_ref[0, blk * tq:(blk + 1) * tq, :] = _rms(o, g, SUBLN_EPS).astype(o_ref.dtype)

    scores(0, 0)
    for blk in range(nb):
        if blk + 1 < nb:
            scores(blk + 1, (blk + 1) % 2)
        values(blk, blk % 2)


def _alibi_roles():
    terms = []
    rest = LOG2E
    for _ in range(3):
        t = float(np.float32(rest).astype(ml_dtypes.bfloat16))
        terms.append(t)
        rest -= t
    roles = np.zeros((2, 6, LANES), np.float32)
    for mi in range(2):
        base = DIFF_D if mi == 0 else 0
        for t, lt in enumerate(terms):
            roles[mi, 0, base + t] = -lt
            roles[mi, 0, base + 3 + t] = -lt
            roles[mi, 1, base + 6 + t] = 1.0
            roles[mi, 2, base + 9 + t] = 1.0
            roles[mi, 3, base + t] = 1.0
            roles[mi, 4, base + 3 + t] = 1.0
            roles[mi, 5, base + 6 + t] = lt
            roles[mi, 5, base + 9 + t] = lt
    return jnp.asarray(roles)


def _diff_call(lam_p, gsub, dq, dk, dv, lam_init):
    batch, _, seq, _ = dq.shape
    roles = _alibi_roles()
    in_spec = pl.BlockSpec((1, 1, seq, LANES), lambda b, h: (b, h, 0, 0))
    return pl.pallas_call(
        functools.partial(_diff_kernel, seq=seq, lam_init=lam_init),
        grid=(batch, DIFF_HEADS),
        in_specs=[_const_spec(lam_p.shape), _const_spec(gsub.shape), _const_spec(roles.shape),
                  in_spec, in_spec, in_spec],
        out_specs=pl.BlockSpec((1, seq, LANES), lambda b, h: (b, 0, h)),
        out_shape=jax.ShapeDtypeStruct((batch, seq, DIFF_HEADS * LANES), BF16),
        scratch_shapes=[pltpu.VMEM((DIFF_HEADS, TQ_DIFF, KEY_TILE), F32),
                        pltpu.VMEM((DIFF_HEADS, 2, seq, LANES), BF16),
                        pltpu.VMEM((2, seq, LANES), BF16),
                        pltpu.VMEM((seq, 2 * LANES), BF16),
                        pltpu.VMEM((2, 2, TQ_DIFF, seq), F32),
                        pltpu.VMEM((2, 2, TQ_DIFF, LANES), F32)],
        compiler_params=pltpu.CompilerParams(
            dimension_semantics=("arbitrary", "arbitrary"), vmem_limit_bytes=VMEM_LIMIT),
        name="diff_attn",
    )(lam_p, gsub, roles, dq, dk, dv)


def _mlp_kernel(om_ref, od_ref, h_ref, wo1_ref, wo2_ref, gmlp_ref, wup_ref, wdn_ref, gfin_ref,
                out_ref, *, final):
    a = h_ref[...] + _dot(om_ref[...], wo1_ref[...]) + _dot(od_ref[...], wo2_ref[...])
    u = _rms(a, gmlp_ref[...], EPS).astype(BF16)
    out_ref[...] = a
    for c in range(D_FF // FF_CHUNK):
        f = jnp.maximum(_dot(u, wup_ref[:, c * FF_CHUNK:(c + 1) * FF_CHUNK]), 0.0)
        out_ref[...] += _dot((f * f).astype(BF16), wdn_ref[c * FF_CHUNK:(c + 1) * FF_CHUNK, :])
    if final:
        out_ref[...] = _rms(out_ref[...], gfin_ref[...], EPS)


def _mlp_call(om, od, hf, wo1, wo2, gmlp, wup, wdn, gfin, final):
    tm = TM_MLP
    tokens = hf.shape[0]
    row = lambda w: pl.BlockSpec((tm, w), lambda i: (i, 0))
    return pl.pallas_call(
        functools.partial(_mlp_kernel, final=final),
        grid=(tokens // tm,),
        in_specs=[row(om.shape[1]), row(od.shape[1]), row(D_MODEL),
                  _const_spec(wo1.shape), _const_spec(wo2.shape), _const_spec(gmlp.shape),
                  _const_spec(wup.shape), _const_spec(wdn.shape), _const_spec(gfin.shape)],
        out_specs=row(D_MODEL),
        out_shape=jax.ShapeDtypeStruct(hf.shape, F32),
        compiler_params=pltpu.CompilerParams(
            dimension_semantics=("arbitrary",), vmem_limit_bytes=VMEM_LIMIT),
        name="outproj_mlp",
    )(om, od, hf, wo1, wo2, gmlp, wup, wdn, gfin)


def _pack_w_in(w):
    zeros = lambda n: jnp.zeros((D_MODEL, n), w.dtype)
    off_kr = Q_LORA + KV_LORA
    off_dq = off_kr + MLA_ROPE
    kr = jnp.concatenate([zeros(MLA_NOPE), w[:, off_kr:off_dq], zeros(LANES - MLA_NOPE - MLA_ROPE)], 1)
    return jnp.concatenate([w[:, :off_kr], kr, w[:, off_dq:]], axis=1).astype(BF16)


def _pack_w_uq(w):
    hd = MLA_NOPE + MLA_ROPE
    w = w.reshape(Q_LORA, MLA_HEADS, hd)
    w = jnp.pad(w, ((0, 0), (0, 0), (0, LANES - hd)))
    return w.reshape(Q_LORA, MLA_HEADS * LANES).astype(BF16)


def _pack_w_ukv(w):
    w = w.reshape(KV_LORA, MLA_HEADS, MLA_NOPE + MLA_V)
    wk = jnp.pad(w[:, :, :MLA_NOPE], ((0, 0), (0, 0), (0, LANES - MLA_NOPE)))
    wv = w[:, :, MLA_NOPE:]
    pad = jnp.zeros_like(wv)
    wv_even = jnp.concatenate([wv, pad], axis=-1)
    wv_odd = jnp.concatenate([pad, wv], axis=-1)
    odd = (jnp.arange(MLA_HEADS) % 2 == 1)[None, :, None]
    wv = jnp.where(odd, wv_odd, wv_even)
    return jnp.concatenate([wk.reshape(KV_LORA, -1), wv.reshape(KV_LORA, -1)], axis=1).astype(BF16)


def _rope_tables(seq):
    pos = np.arange(seq, dtype=np.float32)
    inv_freq = (1.0 / (ROPE_THETA ** (np.arange(0, MLA_ROPE, 2, dtype=np.float32) / MLA_ROPE))).astype(np.float32)
    ang = jnp.asarray(pos[:, None] * inv_freq[None, :])
    cos, sin = jnp.cos(ang), jnp.sin(ang)
    half = MLA_ROPE // 2
    ones = jnp.ones((seq, MLA_NOPE), F32)
    z = lambda n: jnp.zeros((seq, n), F32)
    tail = LANES - MLA_NOPE - MLA_ROPE
    cos_t = jnp.concatenate([ones, cos, cos, z(tail)], axis=1)
    sa_t = jnp.concatenate([z(MLA_NOPE), -sin, z(half), z(tail)], axis=1)
    sb_t = jnp.concatenate([z(MLA_NOPE), z(half), sin, z(tail)], axis=1)
    return cos_t, sa_t, sb_t


def _v_ones():
    v = np.zeros((1, MLA_HEADS * LANES), np.float32)
    for h in range(MLA_HEADS):
        v[0, h * LANES + (MLA_V if h % 2 == 0 else 0)] = 1.0
    return jnp.asarray(v)


def kernel(x, w_in, g_mix, g_q, g_kv, w_uq, w_ukv, lam_q1, lam_k1, lam_q2, lam_k2,
           g_sub, w_out, g_mlp, w_up, w_down, g_final):
    batch, seq, _ = x.shape
    depth = w_in.shape[0]
    cos_t, sa_t, sb_t = _rope_tables(seq)
    vone = _v_ones()
    hf = x.reshape(batch * seq, D_MODEL)
    for l in range(depth):
        lam_init = 0.8 - 0.6 * math.exp(-0.3 * l)
        qm, km, vm, dq, dk, dv = _proj_call(
            hf, g_mix[l][None], _pack_w_in(w_in[l]), g_q[l][None], g_kv[l][None],
            _pack_w_uq(w_uq[l]), _pack_w_ukv(w_ukv[l]), cos_t, sa_t, sb_t, vone, batch, seq)
        o_mla = _mla_call(qm, km, vm)
        lam_p = jnp.stack([lam_q1[l], lam_k1[l], lam_q2[l], lam_k2[l]]).astype(F32)
        o_diff = _diff_call(lam_p, g_sub[l][None], dq, dk, dv, lam_init)
        wo = w_out[l].astype(BF16)
        n_mla = MLA_HEADS * MLA_V
        hf = _mlp_call(o_mla.reshape(batch * seq, -1), o_diff.reshape(batch * seq, -1), hf,
                       wo[:n_mla], wo[n_mla:], g_mlp[l][None], w_up[l].astype(BF16),
                       w_down[l].astype(BF16), g_final[None], final=(l == depth - 1))
    return hf.reshape(batch, seq, D_MODEL)
```

```python
import functools
import math

import ml_dtypes
import numpy as np
import jax
import jax.numpy as jnp
from jax import lax
from jax.experimental import pallas as pl
from jax.experimental.pallas import tpu as pltpu

D_MODEL = 1024
MLA_HEADS = 8
MLA_NOPE = 64
MLA_ROPE = 32
MLA_V = 64
Q_LORA = 256
KV_LORA = 256
DIFF_HEADS = 4
DIFF_D = 64
D_FF = 4 * D_MODEL
ROPE_THETA = 10000.0
EPS = 1e-6
SUBLN_EPS = 1e-5

LANES = 128
LOG2E = math.log2(math.e)

C_CQ = 0
C_CKV = C_CQ + Q_LORA
C_KR = C_CKV + KV_LORA
C_DQ = C_KR + LANES
C_DK = C_DQ + DIFF_HEADS * LANES
C_DV = C_DK + DIFF_HEADS * LANES
P_PACK = C_DV + DIFF_HEADS * LANES

TM_PROJ = 1024
TM_MLP = 1024
TQ_MLA = 256
TQ_DIFF = 256
FF_CHUNK = 1024
DIFF_HEADS_PER_STEP = 2
MLA_PAIRS_PER_STEP = 2
KEY_TILE = 256
VMEM_LIMIT = 56 * 1024 * 1024

F32 = jnp.float32
BF16 = jnp.bfloat16


def _rms(x, g, eps):
    return x * lax.rsqrt(jnp.mean(x * x, axis=-1, keepdims=True) + eps) * g


def _dot(a, b):
    return jnp.dot(a, b, preferred_element_type=F32)


def _dot_nt(a, b):
    return lax.dot_general(a, b, (((1,), (1,)), ((), ())), preferred_element_type=F32)


def _proj_kernel(x_ref, gmix_ref, w1_ref, gq_ref, gkv_ref, wq_ref, wkv_ref,
                 cos_ref, sa_ref, sb_ref, vone_ref,
                 qm_ref, km_ref, vm_ref, dq_ref, dk_ref, dv_ref):
    x = x_ref[...]
    u = _rms(x, gmix_ref[...], EPS).astype(BF16)
    z = _dot(u, w1_ref[...])
    cq = _rms(z[:, C_CQ:C_CKV], gq_ref[...], EPS).astype(BF16)
    ckv = _rms(z[:, C_CKV:C_KR], gkv_ref[...], EPS).astype(BF16)
    q = _dot(cq, wq_ref[...])
    kv = _dot(ckv, wkv_ref[...])
    cos = cos_ref[...]
    sa = sa_ref[...]
    sb = sb_ref[...]

    half = MLA_ROPE // 2

    def rope(t):
        return t * cos + pltpu.roll(t, LANES - half, 1) * sa + pltpu.roll(t, half, 1) * sb

    k_rope = rope(z[:, C_KR:C_DQ])
    q_scale = (MLA_NOPE + MLA_ROPE) ** -0.5 * LOG2E
    nk = MLA_HEADS * LANES
    v_all = kv[:, nk:] + vone_ref[...]
    for h in range(MLA_HEADS):
        sl = slice(h * LANES, (h + 1) * LANES)
        qm_ref[0, h] = (rope(q[:, sl]) * q_scale).astype(BF16)
        km_ref[0, h] = (kv[:, sl] + k_rope).astype(BF16)
        vm_ref[0, h] = v_all[:, sl].astype(BF16)
    d_scale = DIFF_D ** -0.5 * LOG2E
    for h in range(DIFF_HEADS):
        dq_ref[0, h] = (z[:, C_DQ + h * LANES:C_DQ + (h + 1) * LANES] * d_scale).astype(BF16)
        dk_ref[0, h] = z[:, C_DK + h * LANES:C_DK + (h + 1) * LANES].astype(BF16)
        dv_ref[0, h] = z[:, C_DV + h * LANES:C_DV + (h + 1) * LANES].astype(BF16)


def _const_spec(shape):
    nd = len(shape)
    return pl.BlockSpec(shape, lambda *_: (0,) * nd, pipeline_mode=pl.Buffered(1))


def _proj_call(xf, gmix, w1, gq, gkv, wq, wkv, cos_t, sa_t, sb_t, vone, batch, seq):
    tm = TM_PROJ
    nsb = seq // tm
    tokens = batch * seq
    head_spec = lambda nh: pl.BlockSpec((1, nh, tm, LANES), lambda i: (i // nsb, 0, i % nsb, 0))
    tab_spec = pl.BlockSpec((tm, LANES), lambda i: (i % nsb, 0))
    mshape = jax.ShapeDtypeStruct((batch, MLA_HEADS, seq, LANES), BF16)
    dshape = jax.ShapeDtypeStruct((batch, DIFF_HEADS, seq, LANES), BF16)
    return pl.pallas_call(
        _proj_kernel,
        grid=(tokens // tm,),
        in_specs=[
            pl.BlockSpec((tm, D_MODEL), lambda i: (i, 0)),
            _const_spec(gmix.shape), _const_spec(w1.shape), _const_spec(gq.shape),
            _const_spec(gkv.shape), _const_spec(wq.shape), _const_spec(wkv.shape),
            tab_spec, tab_spec, tab_spec, _const_spec(vone.shape),
        ],
        out_specs=[head_spec(MLA_HEADS)] * 3 + [head_spec(DIFF_HEADS)] * 3,
        out_shape=[mshape] * 3 + [dshape] * 3,
        compiler_params=pltpu.CompilerParams(
            dimension_semantics=("arbitrary",), vmem_limit_bytes=VMEM_LIMIT),
        name="proj",
    )(xf, gmix, w1, gq, gkv, wq, wkv, cos_t, sa_t, sb_t, vone)


def _mla_kernel(q_ref, k_ref, v_ref, o_ref, s_ref, m_ref, *, seq):
    tq = TQ_MLA
    nb = seq // tq
    nkt = seq // KEY_TILE
    lane = lax.broadcasted_iota(jnp.int32, (tq, LANES), 1)

    def scores(unit, slot):
        pair, blk = divmod(unit, nb)
        for hh in range(2):
            q = q_ref[0, 2 * pair + hh, blk * tq:(blk + 1) * tq, :]
            macc = None
            for kt in range(nkt):
                cols = slice(kt * KEY_TILE, (kt + 1) * KEY_TILE)
                s = _dot_nt(q, k_ref[0, 2 * pair + hh, cols, :])
                s_ref[slot, hh, :, cols] = s
                t = jnp.maximum(s[:, :LANES], s[:, LANES:])
                macc = t if macc is None else jnp.maximum(macc, t)
            m = jnp.max(macc, axis=-1, keepdims=True)
            m_ref[slot, hh] = jnp.broadcast_to(m, (tq, LANES))

    def values(unit, slot):
        pair, blk = divmod(unit, nb)
        acc = None
        for kt in range(nkt):
            cols = slice(kt * KEY_TILE, (kt + 1) * KEY_TILE)
            ps = []
            for hh in range(2):
                mb = m_ref[slot, hh]
                p = jnp.exp2(s_ref[slot, hh, :, cols] - jnp.concatenate([mb, mb], axis=1))
                ps.append(p.astype(BF16))
            vt = jnp.concatenate([v_ref[0, 2 * pair, cols, :], v_ref[0, 2 * pair + 1, cols, :]],
                                 axis=1)
            part = _dot(jnp.concatenate(ps, axis=0), vt)
            acc = part if acc is None else acc + part
        outs = []
        for hh in range(2):
            a = acc[hh * tq:(hh + 1) * tq, hh * LANES:(hh + 1) * LANES]
            lcol = MLA_V if hh == 0 else 0
            outs.append(a * (1.0 / a[:, lcol:lcol + 1]))
        o_ref[0, blk * tq:(blk + 1) * tq, pair * LANES:(pair + 1) * LANES] = (
            jnp.where(lane < MLA_V, outs[0], outs[1]).astype(o_ref.dtype))

    units = MLA_PAIRS_PER_STEP * nb
    scores(0, 0)
    for unit in range(units):
        if unit + 1 < units:
            scores(unit + 1, (unit + 1) % 2)
        values(unit, unit % 2)


def _mla_call(qm, km, vm):
    batch, _, seq, _ = qm.shape
    hps = 2 * MLA_PAIRS_PER_STEP
    in_spec = pl.BlockSpec((1, hps, seq, LANES), lambda b, g: (b, g, 0, 0))
    return pl.pallas_call(
        functools.partial(_mla_kernel, seq=seq),
        grid=(batch, MLA_HEADS // hps),
        in_specs=[in_spec, in_spec, in_spec],
        out_specs=pl.BlockSpec((1, seq, MLA_PAIRS_PER_STEP * LANES), lambda b, g: (b, 0, g)),
        out_shape=jax.ShapeDtypeStruct((batch, seq, MLA_HEADS * MLA_V), BF16),
        scratch_shapes=[pltpu.VMEM((2, 2, TQ_MLA, seq), F32),
                        pltpu.VMEM((2, 2, TQ_MLA, LANES), F32)],
        compiler_params=pltpu.CompilerParams(
            dimension_semantics=("arbitrary", "arbitrary"), vmem_limit_bytes=VMEM_LIMIT),
        name="mla_attn",
    )(qm, km, vm)


def _diff_kernel(lam_ref, gsub_ref, roles_ref, q_ref, k_ref, v_ref, o_ref,
                 bias_ref, kfeat_ref, k12_ref, vaug_ref, s_ref, m_ref, *, seq, lam_init):
    tq = TQ_DIFF
    b = pl.program_id(0)
    g_idx = pl.program_id(1)
    nb = seq // tq
    nkt = seq // KEY_TILE
    assert tq == KEY_TILE
    hps = DIFF_HEADS_PER_STEP

    slopes = []
    for hd in range(hps):
        hv = jnp.full((1, 1), g_idx * hps + hd, jnp.int32)
        slope = jnp.zeros((1, 1), F32)
        for hh in range(DIFF_HEADS):
            slope = jnp.where(hv == hh, 2.0 ** (-8.0 * (hh + 1.0) / DIFF_HEADS), slope)
        slopes.append(slope)

    @pl.when(b == 0)
    def _():
        r = lax.broadcasted_iota(jnp.int32, (tq, KEY_TILE), 0)
        c = lax.broadcasted_iota(jnp.int32, (tq, KEY_TILE), 1)
        dist = jnp.abs(r - c).astype(F32)
        j = lax.broadcasted_iota(jnp.int32, (seq, LANES), 0)
        j_tile = (j - (j & (KEY_TILE - 1))).astype(F32)
        j_in = (j & (KEY_TILE - 1)).astype(F32)
        for hd in range(hps):
            h = g_idx * hps + hd
            bias_ref[h] = dist * (-LOG2E * slopes[hd])
            for mi in range(2):
                kf = (slopes[hd] * roles_ref[mi, 0:1] + roles_ref[mi, 1:2] * j_tile
                      + roles_ref[mi, 2:3] * j_in)
                kfeat_ref[h, mi] = kf.astype(BF16)

    lp = lam_ref[...]
    lam = (jnp.exp(jnp.sum(lp[0:1] * lp[1:2], axis=-1, keepdims=True))
           - jnp.exp(jnp.sum(lp[2:3] * lp[3:4], axis=-1, keepdims=True)) + lam_init)

    lane_k = lax.broadcasted_iota(jnp.int32, (seq, LANES), 1)
    for hd in range(hps):
        h = g_idx * hps + hd
        k = k_ref[0, hd]
        k12_ref[hd, 0] = jnp.where(lane_k < DIFF_D, k, kfeat_ref[h, 0])
        k12_ref[hd, 1] = jnp.where(lane_k >= DIFF_D, k, kfeat_ref[h, 1])
        vaug_ref[hd, :, :LANES] = v_ref[0, hd]
        vaug_ref[hd, :, LANES:] = (lane_k == 0).astype(BF16)
    g = gsub_ref[...] * (1.0 - lam_init)
    lane_q = lax.broadcasted_iota(jnp.int32, (tq, LANES), 1)
    i_in = lax.broadcasted_iota(jnp.int32, (tq, LANES), 0).astype(F32)

    def scores(unit, slot):
        hd, blk = divmod(unit, nb)
        q = q_ref[0, hd, blk * tq:(blk + 1) * tq, :]
        for mi in range(2):
            data = (lane_q < DIFF_D) if mi == 0 else (lane_q >= DIFF_D)
            qf = (roles_ref[mi, 3:4] * float(blk * tq) + roles_ref[mi, 4:5] * i_in
                  + slopes[hd] * roles_ref[mi, 5:6])
            lhs = {0: jnp.where(data, q, jnp.zeros_like(q))}
            if blk > 0:
                lhs[1] = jnp.where(data, q, qf.astype(BF16))
            if blk < nb - 1:
                lhs[-1] = jnp.where(data, q, (-qf).astype(BF16))
            macc = None
            for kt in range(nkt):
                cols = slice(kt * KEY_TILE, (kt + 1) * KEY_TILE)
                s = _dot_nt(lhs[(blk > kt) - (blk < kt)], k12_ref[hd, mi, cols, :])
                if kt == blk:
                    s = s + bias_ref[g_idx * hps + hd]
                s_ref[slot, mi, :, cols] = s
                t = jnp.maximum(s[:, :LANES], s[:, LANES:])
                macc = t if macc is None else jnp.maximum(macc, t)
            m = jnp.max(macc, axis=-1, keepdims=True)
            m_ref[slot, mi] = jnp.broadcast_to(m, (tq, LANES))

    def values(unit, slot):
        hd, blk = divmod(unit, nb)
        outs = []
        for mi in range(2):
            acc = None
            for kt in range(nkt):
                cols = slice(kt * KEY_TILE, (kt + 1) * KEY_TILE)
                mb = m_ref[slot, mi]
                p = jnp.exp2(s_ref[slot, mi, :, cols] - jnp.concatenate([mb, mb], axis=1))
                part = _dot(p.astype(BF16), vaug_ref[hd, cols, :])
                acc = part if acc is None else acc + part
            outs.append(acc[:, :LANES] * (1.0 / acc[:, LANES:LANES + 1]))
        o = outs[0] - lam * outs[1]
        o_ref[0, blk * tq:(blk + 1) * tq, hd * LANES:(hd + 1) * LANES] = (
            _rms(o, g, SUBLN_EPS).astype(o_ref.dtype))

    units = hps * nb
    scores(0, 0)
    for unit in range(units):
        if unit + 1 < units:
            scores(unit + 1, (unit + 1) % 2)
        values(unit, unit % 2)


def _alibi_roles():
    terms = []
    rest = LOG2E
    for _ in range(3):
        t = float(np.float32(rest).astype(ml_dtypes.bfloat16))
        terms.append(t)
        rest -= t
    roles = np.zeros((2, 6, LANES), np.float32)
    for mi in range(2):
        base = DIFF_D if mi == 0 else 0
        for t, lt in enumerate(terms):
            roles[mi, 0, base + t] = -lt
            roles[mi, 0, base + 3 + t] = -lt
            roles[mi, 1, base + 6 + t] = 1.0
            roles[mi, 2, base + 9 + t] = 1.0
            roles[mi, 3, base + t] = 1.0
            roles[mi, 4, base + 3 + t] = 1.0
            roles[mi, 5, base + 6 + t] = lt
            roles[mi, 5, base + 9 + t] = lt
    return jnp.asarray(roles)


def _diff_call(lam_p, gsub, dq, dk, dv, lam_init):
    batch, _, seq, _ = dq.shape
    roles = _alibi_roles()
    hps = DIFF_HEADS_PER_STEP
    in_spec = pl.BlockSpec((1, hps, seq, LANES), lambda b, g: (b, g, 0, 0))
    return pl.pallas_call(
        functools.partial(_diff_kernel, seq=seq, lam_init=lam_init),
        grid=(batch, DIFF_HEADS // hps),
        in_specs=[_const_spec(lam_p.shape), _const_spec(gsub.shape), _const_spec(roles.shape),
                  in_spec, in_spec, in_spec],
        out_specs=pl.BlockSpec((1, seq, hps * LANES), lambda b, g: (b, 0, g)),
        out_shape=jax.ShapeDtypeStruct((batch, seq, DIFF_HEADS * LANES), BF16),
        scratch_shapes=[pltpu.VMEM((DIFF_HEADS, TQ_DIFF, KEY_TILE), F32),
                        pltpu.VMEM((DIFF_HEADS, 2, seq, LANES), BF16),
                        pltpu.VMEM((hps, 2, seq, LANES), BF16),
                        pltpu.VMEM((hps, seq, 2 * LANES), BF16),
                        pltpu.VMEM((2, 2, TQ_DIFF, seq), F32),
                        pltpu.VMEM((2, 2, TQ_DIFF, LANES), F32)],
        compiler_params=pltpu.CompilerParams(
            dimension_semantics=("arbitrary", "arbitrary"), vmem_limit_bytes=VMEM_LIMIT),
        name="diff_attn",
    )(lam_p, gsub, roles, dq, dk, dv)


def _mlp_kernel(om_ref, od_ref, h_ref, wo1_ref, wo2_ref, gmlp_ref, wup_ref, wdn_ref, gfin_ref,
                out_ref, *, final):
    a = h_ref[...] + _dot(om_ref[...], wo1_ref[...]) + _dot(od_ref[...], wo2_ref[...])
    u = _rms(a, gmlp_ref[...], EPS).astype(BF16)
    out_ref[...] = a
    for c in range(D_FF // FF_CHUNK):
        f = jnp.maximum(_dot(u, wup_ref[:, c * FF_CHUNK:(c + 1) * FF_CHUNK]), 0.0)
        out_ref[...] += _dot((f * f).astype(BF16), wdn_ref[c * FF_CHUNK:(c + 1) * FF_CHUNK, :])
    if final:
        out_ref[...] = _rms(out_ref[...], gfin_ref[...], EPS)


def _mlp_call(om, od, hf, wo1, wo2, gmlp, wup, wdn, gfin, final):
    tm = TM_MLP
    tokens = hf.shape[0]
    row = lambda w: pl.BlockSpec((tm, w), lambda i: (i, 0))
    return pl.pallas_call(
        functools.partial(_mlp_kernel, final=final),
        grid=(tokens // tm,),
        in_specs=[row(om.shape[1]), row(od.shape[1]), row(D_MODEL),
                  _const_spec(wo1.shape), _const_spec(wo2.shape), _const_spec(gmlp.shape),
                  _const_spec(wup.shape), _const_spec(wdn.shape), _const_spec(gfin.shape)],
        out_specs=row(D_MODEL),
        out_shape=jax.ShapeDtypeStruct(hf.shape, F32),
        compiler_params=pltpu.CompilerParams(
            dimension_semantics=("arbitrary",), vmem_limit_bytes=VMEM_LIMIT),
        name="outproj_mlp",
    )(om, od, hf, wo1, wo2, gmlp, wup, wdn, gfin)


def _pack_w_in(w):
    zeros = lambda n: jnp.zeros((D_MODEL, n), w.dtype)
    off_kr = Q_LORA + KV_LORA
    off_dq = off_kr + MLA_ROPE
    kr = jnp.concatenate([zeros(MLA_NOPE), w[:, off_kr:off_dq], zeros(LANES - MLA_NOPE - MLA_ROPE)], 1)
    return jnp.concatenate([w[:, :off_kr], kr, w[:, off_dq:]], axis=1).astype(BF16)


def _pack_w_uq(w):
    hd = MLA_NOPE + MLA_ROPE
    w = w.reshape(Q_LORA, MLA_HEADS, hd)
    w = jnp.pad(w, ((0, 0), (0, 0), (0, LANES - hd)))
    return w.reshape(Q_LORA, MLA_HEADS * LANES).astype(BF16)


def _pack_w_ukv(w):
    w = w.reshape(KV_LORA, MLA_HEADS, MLA_NOPE + MLA_V)
    wk = jnp.pad(w[:, :, :MLA_NOPE], ((0, 0), (0, 0), (0, LANES - MLA_NOPE)))
    wv = w[:, :, MLA_NOPE:]
    pad = jnp.zeros_like(wv)
    wv_even = jnp.concatenate([wv, pad], axis=-1)
    wv_odd = jnp.concatenate([pad, wv], axis=-1)
    odd = (jnp.arange(MLA_HEADS) % 2 == 1)[None, :, None]
    wv = jnp.where(odd, wv_odd, wv_even)
    return jnp.concatenate([wk.reshape(KV_LORA, -1), wv.reshape(KV_LORA, -1)], axis=1).astype(BF16)


def _rope_tables(seq):
    pos = np.arange(seq, dtype=np.float32)
    inv_freq = (1.0 / (ROPE_THETA ** (np.arange(0, MLA_ROPE, 2, dtype=np.float32) / MLA_ROPE))).astype(np.float32)
    ang = jnp.asarray(pos[:, None] * inv_freq[None, :])
    cos, sin = jnp.cos(ang), jnp.sin(ang)
    half = MLA_ROPE // 2
    ones = jnp.ones((seq, MLA_NOPE), F32)
    z = lambda n: jnp.zeros((seq, n), F32)
    tail = LANES - MLA_NOPE - MLA_ROPE
    cos_t = jnp.concatenate([ones, cos, cos, z(tail)], axis=1)
    sa_t = jnp.concatenate([z(MLA_NOPE), -sin, z(half), z(tail)], axis=1)
    sb_t = jnp.concatenate([z(MLA_NOPE), z(half), sin, z(tail)], axis=1)
    return cos_t, sa_t, sb_t


def _v_ones():
    v = np.zeros((1, MLA_HEADS * LANES), np.float32)
    for h in range(MLA_HEADS):
        v[0, h * LANES + (MLA_V if h % 2 == 0 else 0)] = 1.0
    return jnp.asarray(v)


def kernel(x, w_in, g_mix, g_q, g_kv, w_uq, w_ukv, lam_q1, lam_k1, lam_q2, lam_k2,
           g_sub, w_out, g_mlp, w_up, w_down, g_final):
    batch, seq, _ = x.shape
    depth = w_in.shape[0]
    cos_t, sa_t, sb_t = _rope_tables(seq)
    vone = _v_ones()
    hf = x.reshape(batch * seq, D_MODEL)
    for l in range(depth):
        lam_init = 0.8 - 0.6 * math.exp(-0.3 * l)
        qm, km, vm, dq, dk, dv = _proj_call(
            hf, g_mix[l][None], _pack_w_in(w_in[l]), g_q[l][None], g_kv[l][None],
            _pack_w_uq(w_uq[l]), _pack_w_ukv(w_ukv[l]), cos_t, sa_t, sb_t, vone, batch, seq)
        o_mla = _mla_call(qm, km, vm)
        lam_p = jnp.stack([lam_q1[l], lam_k1[l], lam_q2[l], lam_k2[l]]).astype(F32)
        o_diff = _diff_call(lam_p, g_sub[l][None], dq, dk, dv, lam_init)
        wo = w_out[l].astype(BF16)
        n_mla = MLA_HEADS * MLA_V
        hf = _mlp_call(o_mla.reshape(batch * seq, -1), o_diff.reshape(batch * seq, -1), hf,
                       wo[:n_mla], wo[n_mla:], g_mlp[l][None], w_up[l].astype(BF16),
                       w_down[l].astype(BF16), g_final[None], final=(l == depth - 1))
    return hf.reshape(batch, seq, D_MODEL)
```

```python
import functools
import math

import ml_dtypes
import numpy as np
import jax
import jax.numpy as jnp
from jax import lax
from jax.experimental import pallas as pl
from jax.experimental.pallas import tpu as pltpu

D_MODEL = 1024
MLA_HEADS = 8
MLA_NOPE = 64
MLA_ROPE = 32
MLA_V = 64
Q_LORA = 256
KV_LORA = 256
DIFF_HEADS = 4
DIFF_D = 64
D_FF = 4 * D_MODEL
ROPE_THETA = 10000.0
EPS = 1e-6
SUBLN_EPS = 1e-5

LANES = 128
LOG2E = math.log2(math.e)

C_CQ = 0
C_CKV = C_CQ + Q_LORA
C_KR = C_CKV + KV_LORA
C_DQ = C_KR + LANES
C_DK = C_DQ + DIFF_HEADS * LANES
C_DV = C_DK + DIFF_HEADS * LANES
P_PACK = C_DV + DIFF_HEADS * LANES

TM_PROJ = 1024
TM_MLP = 1024
TQ_MLA = 256
TQ_DIFF = 256
FF_CHUNK = 1024
DIFF_HEADS_PER_STEP = 2
MLA_PAIRS_PER_STEP = 2
KEY_TILE = 256
VMEM_LIMIT = 56 * 1024 * 1024

F32 = jnp.float32
BF16 = jnp.bfloat16


def _rms(x, g, eps):
    return x * lax.rsqrt(jnp.mean(x * x, axis=-1, keepdims=True) + eps) * g


def _dot(a, b):
    return jnp.dot(a, b, preferred_element_type=F32)


def _dot_nt(a, b):
    return lax.dot_general(a, b, (((1,), (1,)), ((), ())), preferred_element_type=F32)


def _proj_kernel(x_ref, gmix_ref, w1_ref, gq_ref, gkv_ref, wq_ref, wkv_ref,
                 cos_ref, sa_ref, sb_ref, vone_ref,
                 qm_ref, km_ref, vm_ref, dq_ref, dk_ref, dv_ref):
    x = x_ref[...]
    u = _rms(x, gmix_ref[...], EPS).astype(BF16)
    z = _dot(u, w1_ref[...])
    cq = _rms(z[:, C_CQ:C_CKV], gq_ref[...], EPS).astype(BF16)
    ckv = _rms(z[:, C_CKV:C_KR], gkv_ref[...], EPS).astype(BF16)
    q = _dot(cq, wq_ref[...])
    kv = _dot(ckv, wkv_ref[...])
    cos = cos_ref[...]
    sa = sa_ref[...]
    sb = sb_ref[...]

    half = MLA_ROPE // 2

    def rope(t):
        return t * cos + pltpu.roll(t, LANES - half, 1) * sa + pltpu.roll(t, half, 1) * sb

    k_rope = rope(z[:, C_KR:C_DQ])
    q_scale = (MLA_NOPE + MLA_ROPE) ** -0.5 * LOG2E
    nk = MLA_HEADS * LANES
    v_all = kv[:, nk:] + vone_ref[...]
    for h in range(MLA_HEADS):
        sl = slice(h * LANES, (h + 1) * LANES)
        qm_ref[0, h] = (rope(q[:, sl]) * q_scale).astype(BF16)
        km_ref[0, h] = (kv[:, sl] + k_rope).astype(BF16)
        vm_ref[0, h] = v_all[:, sl].astype(BF16)
    d_scale = DIFF_D ** -0.5 * LOG2E
    for h in range(DIFF_HEADS):
        dq_ref[0, h] = (z[:, C_DQ + h * LANES:C_DQ + (h + 1) * LANES] * d_scale).astype(BF16)
        dk_ref[0, h] = z[:, C_DK + h * LANES:C_DK + (h + 1) * LANES].astype(BF16)
        dv_ref[0, h] = z[:, C_DV + h * LANES:C_DV + (h + 1) * LANES].astype(BF16)


def _const_spec(shape):
    nd = len(shape)
    return pl.BlockSpec(shape, lambda *_: (0,) * nd, pipeline_mode=pl.Buffered(1))


def _proj_call(xf, gmix, w1, gq, gkv, wq, wkv, cos_t, sa_t, sb_t, vone, batch, seq):
    tm = TM_PROJ
    nsb = seq // tm
    tokens = batch * seq
    head_spec = lambda nh: pl.BlockSpec((1, nh, tm, LANES), lambda i: (i // nsb, 0, i % nsb, 0))
    tab_spec = pl.BlockSpec((tm, LANES), lambda i: (i % nsb, 0))
    mshape = jax.ShapeDtypeStruct((batch, MLA_HEADS, seq, LANES), BF16)
    dshape = jax.ShapeDtypeStruct((batch, DIFF_HEADS, seq, LANES), BF16)
    return pl.pallas_call(
        _proj_kernel,
        grid=(tokens // tm,),
        in_specs=[
            pl.BlockSpec((tm, D_MODEL), lambda i: (i, 0)),
            _const_spec(gmix.shape), _const_spec(w1.shape), _const_spec(gq.shape),
            _const_spec(gkv.shape), _const_spec(wq.shape), _const_spec(wkv.shape),
            tab_spec, tab_spec, tab_spec, _const_spec(vone.shape),
        ],
        out_specs=[head_spec(MLA_HEADS)] * 3 + [head_spec(DIFF_HEADS)] * 3,
        out_shape=[mshape] * 3 + [dshape] * 3,
        compiler_params=pltpu.CompilerParams(
            dimension_semantics=("arbitrary",), vmem_limit_bytes=VMEM_LIMIT),
        name="proj",
    )(xf, gmix, w1, gq, gkv, wq, wkv, cos_t, sa_t, sb_t, vone)


def _mla_kernel(q_ref, k_ref, v_ref, o_ref, s_ref, m_ref, *, seq):
    tq = TQ_MLA
    nb = seq // tq
    nkt = seq // KEY_TILE
    lane = lax.broadcasted_iota(jnp.int32, (tq, LANES), 1)

    def scores(unit, slot):
        pair, blk = divmod(unit, nb)
        for hh in range(2):
            q = q_ref[0, 2 * pair + hh, blk * tq:(blk + 1) * tq, :]
            macc = None
            for kt in range(nkt):
                cols = slice(kt * KEY_TILE, (kt + 1) * KEY_TILE)
                s = _dot_nt(q, k_ref[0, 2 * pair + hh, cols, :])
                s_ref[slot, hh, :, cols] = s
                t = jnp.maximum(s[:, :LANES], s[:, LANES:])
                macc = t if macc is None else jnp.maximum(macc, t)
            m = jnp.max(macc, axis=-1, keepdims=True)
            m_ref[slot, hh] = jnp.broadcast_to(m, (tq, LANES))

    def values(unit, slot):
        pair, blk = divmod(unit, nb)
        acc = None
        for kt in range(nkt):
            cols = slice(kt * KEY_TILE, (kt + 1) * KEY_TILE)
            ps = []
            for hh in range(2):
                mb = m_ref[slot, hh]
                p = jnp.exp2(s_ref[slot, hh, :, cols] - jnp.concatenate([mb, mb], axis=1))
                ps.append(p.astype(BF16))
            vt = jnp.concatenate([v_ref[0, 2 * pair, cols, :], v_ref[0, 2 * pair + 1, cols, :]],
                                 axis=1)
            part = _dot(jnp.concatenate(ps, axis=0), vt)
            acc = part if acc is None else acc + part
        outs = []
        for hh in range(2):
            a = acc[hh * tq:(hh + 1) * tq, hh * LANES:(hh + 1) * LANES]
            lcol = MLA_V if hh == 0 else 0
            outs.append(a * (1.0 / a[:, lcol:lcol + 1]))
        o_ref[0, blk * tq:(blk + 1) * tq, pair * LANES:(pair + 1) * LANES] = (
            jnp.where(lane < MLA_V, outs[0], outs[1]).astype(o_ref.dtype))

    units = MLA_PAIRS_PER_STEP * nb
    scores(0, 0)
    for unit in range(units):
        if unit + 1 < units:
            scores(unit + 1, (unit + 1) % 2)
        values(unit, unit % 2)


def _mla_call(qm, km, vm):
    batch, _, seq, _ = qm.shape
    hps = 2 * MLA_PAIRS_PER_STEP
    in_spec = pl.BlockSpec((1, hps, seq, LANES), lambda b, g: (b, g, 0, 0))
    return pl.pallas_call(
        functools.partial(_mla_kernel, seq=seq),
        grid=(batch, MLA_HEADS // hps),
        in_specs=[in_spec, in_spec, in_spec],
        out_specs=pl.BlockSpec((1, seq, MLA_PAIRS_PER_STEP * LANES), lambda b, g: (b, 0, g)),
        out_shape=jax.ShapeDtypeStruct((batch, seq, MLA_HEADS * MLA_V), BF16),
        scratch_shapes=[pltpu.VMEM((2, 2, TQ_MLA, seq), F32),
                        pltpu.VMEM((2, 2, TQ_MLA, LANES), F32)],
        compiler_params=pltpu.CompilerParams(
            dimension_semantics=("arbitrary", "arbitrary"), vmem_limit_bytes=VMEM_LIMIT),
        name="mla_attn",
    )(qm, km, vm)


def _diff_kernel(lam_ref, gsub_ref, roles_ref, q_ref, k_ref, v_ref, o_ref,
                 bias_ref, kfeat_ref, k12_ref, vaug_ref, s_ref, m_ref, *, seq, lam_init):
    tq = TQ_DIFF
    b = pl.program_id(0)
    g_idx = pl.program_id(1)
    nb = seq // tq
    nkt = seq // KEY_TILE
    assert tq == KEY_TILE
    hps = DIFF_HEADS_PER_STEP

    slopes = []
    for hd in range(hps):
        hv = jnp.full((1, 1), g_idx * hps + hd, jnp.int32)
        slope = jnp.zeros((1, 1), F32)
        for hh in range(DIFF_HEADS):
            slope = jnp.where(hv == hh, 2.0 ** (-8.0 * (hh + 1.0) / DIFF_HEADS), slope)
        slopes.append(slope)

    @pl.when(b == 0)
    def _():
        r = lax.broadcasted_iota(jnp.int32, (tq, KEY_TILE), 0)
        c = lax.broadcasted_iota(jnp.int32, (tq, KEY_TILE), 1)
        dist = jnp.abs(r - c).astype(F32)
        j = lax.broadcasted_iota(jnp.int32, (seq, LANES), 0)
        j_tile = (j - (j & (KEY_TILE - 1))).astype(F32)
        j_in = (j & (KEY_TILE - 1)).astype(F32)
        for hd in range(hps):
            h = g_idx * hps + hd
            bias_ref[h] = dist * (-LOG2E * slopes[hd])
            for mi in range(2):
                kf = (slopes[hd] * roles_ref[mi, 0:1] + roles_ref[mi, 1:2] * j_tile
                      + roles_ref[mi, 2:3] * j_in)
                kfeat_ref[h, mi] = kf.astype(BF16)

    lp = lam_ref[...]
    lam = (jnp.exp(jnp.sum(lp[0:1] * lp[1:2], axis=-1, keepdims=True))
           - jnp.exp(jnp.sum(lp[2:3] * lp[3:4], axis=-1, keepdims=True)) + lam_init)

    lane_k = lax.broadcasted_iota(jnp.int32, (seq, LANES), 1)
    for hd in range(hps):
        h = g_idx * hps + hd
        k = k_ref[0, hd]
        k12_ref[hd, 0] = jnp.where(lane_k < DIFF_D, k, kfeat_ref[h, 0])
        k12_ref[hd, 1] = jnp.where(lane_k >= DIFF_D, k, kfeat_ref[h, 1])
        vaug_ref[hd, :, :LANES] = v_ref[0, hd]
        vaug_ref[hd, :, LANES:] = (lane_k == 0).astype(BF16)
    g = gsub_ref[...] * (1.0 - lam_init)
    lane_q = lax.broadcasted_iota(jnp.int32, (tq, LANES), 1)
    i_in = lax.broadcasted_iota(jnp.int32, (tq, LANES), 0).astype(F32)

    def scores(unit, slot):
        hd, blk = divmod(unit, nb)
        q = q_ref[0, hd, blk * tq:(blk + 1) * tq, :]
        for mi in range(2):
            data = (lane_q < DIFF_D) if mi == 0 else (lane_q >= DIFF_D)
            qf = (roles_ref[mi, 3:4] * float(blk * tq) + roles_ref[mi, 4:5] * i_in
                  + slopes[hd] * roles_ref[mi, 5:6])
            lhs = {0: jnp.where(data, q, jnp.zeros_like(q))}
            if blk > 0:
                lhs[1] = jnp.where(data, q, qf.astype(BF16))
            if blk < nb - 1:
                lhs[-1] = jnp.where(data, q, (-qf).astype(BF16))
            macc = None
            for kt in range(nkt):
                cols = slice(kt * KEY_TILE, (kt + 1) * KEY_TILE)
                s = _dot_nt(lhs[(blk > kt) - (blk < kt)], k12_ref[hd, mi, cols, :])
                if kt == blk:
                    s = s + bias_ref[g_idx * hps + hd]
                s_ref[slot, mi, :, cols] = s
                t = jnp.maximum(s[:, :LANES], s[:, LANES:])
                macc = t if macc is None else jnp.maximum(macc, t)
            m = jnp.max(macc, axis=-1, keepdims=True)
            m_ref[slot, mi] = jnp.broadcast_to(m, (tq, LANES))

    def values(unit, slot):
        hd, blk = divmod(unit, nb)
        outs = []
        for mi in range(2):
            acc = None
            for kt in range(nkt):
                cols = slice(kt * KEY_TILE, (kt + 1) * KEY_TILE)
                mb = m_ref[slot, mi]
                p = jnp.exp2(s_ref[slot, mi, :, cols] - jnp.concatenate([mb, mb], axis=1))
                part = _dot(p.astype(BF16), vaug_ref[hd, cols, :])
                acc = part if acc is None else acc + part
            outs.append(acc[:, :LANES] * (1.0 / acc[:, LANES:LANES + 1]))
        o = outs[0] - lam * outs[1]
        o_ref[0, blk * tq:(blk + 1) * tq, hd * LANES:(hd + 1) * LANES] = (
            _rms(o, g, SUBLN_EPS).astype(o_ref.dtype))

    units = hps * nb
    scores(0, 0)
    for unit in range(units):
        if unit + 1 < units:
            scores(unit + 1, (unit + 1) % 2)
        values(unit, unit % 2)


def _alibi_roles():
    terms = []
    rest = LOG2E
    for _ in range(3):
        t = float(np.float32(rest).astype(ml_dtypes.bfloat16))
        terms.append(t)
        rest -= t
    roles = np.zeros((2, 6, LANES), np.float32)
    for mi in range(2):
        base = DIFF_D if mi == 0 else 0
        for t, lt in enumerate(terms):
            roles[mi, 0, base + t] = -lt
            roles[mi, 0, base + 3 + t] = -lt
            roles[mi, 1, base + 6 + t] = 1.0
            roles[mi, 2, base + 9 + t] = 1.0
            roles[mi, 3, base + t] = 1.0
            roles[mi, 4, base + 3 + t] = 1.0
            roles[mi, 5, base + 6 + t] = lt
            roles[mi, 5, base + 9 + t] = lt
    return jnp.asarray(roles)


def _diff_call(lam_p, gsub, dq, dk, dv, lam_init):
    batch, _, seq, _ = dq.shape
    roles = _alibi_roles()
    hps = DIFF_HEADS_PER_STEP
    in_spec = pl.BlockSpec((1, hps, seq, LANES), lambda b, g: (b, g, 0, 0))
    return pl.pallas_call(
        functools.partial(_diff_kernel, seq=seq, lam_init=lam_init),
        grid=(batch, DIFF_HEADS // hps),
        in_specs=[_const_spec(lam_p.shape), _const_spec(gsub.shape), _const_spec(roles.shape),
                  in_spec, in_spec, in_spec],
        out_specs=pl.BlockSpec((1, seq, hps * LANES), lambda b, g: (b, 0, g)),
        out_shape=jax.ShapeDtypeStruct((batch, seq, DIFF_HEADS * LANES), BF16),
        scratch_shapes=[pltpu.VMEM((DIFF_HEADS, TQ_DIFF, KEY_TILE), F32),
                        pltpu.VMEM((DIFF_HEADS, 2, seq, LANES), BF16),
                        pltpu.VMEM((hps, 2, seq, LANES), BF16),
                        pltpu.VMEM((hps, seq, 2 * LANES), BF16),
                        pltpu.VMEM((2, 2, TQ_DIFF, seq), F32),
                        pltpu.VMEM((2, 2, TQ_DIFF, LANES), F32)],
        compiler_params=pltpu.CompilerParams(
            dimension_semantics=("arbitrary", "arbitrary"), vmem_limit_bytes=VMEM_LIMIT),
        name="diff_attn",
    )(lam_p, gsub, roles, dq, dk, dv)


def _mlp_kernel(om_ref, od_ref, h_ref, wo_ref, gmlp_ref, wup_ref, wdn_ref, gfin_ref,
                out_ref, *, final):
    n_mla = om_ref.shape[1]
    a = (h_ref[...] + _dot(om_ref[...], wo_ref[:n_mla, :])
         + _dot(od_ref[...], wo_ref[n_mla:, :]))
    u = _rms(a, gmlp_ref[...], EPS).astype(BF16)
    out_ref[...] = a
    for c in range(D_FF // FF_CHUNK):
        f = jnp.maximum(_dot(u, wup_ref[:, c * FF_CHUNK:(c + 1) * FF_CHUNK]), 0.0)
        out_ref[...] += _dot((f * f).astype(BF16), wdn_ref[c * FF_CHUNK:(c + 1) * FF_CHUNK, :])
    if final:
        out_ref[...] = _rms(out_ref[...], gfin_ref[...], EPS)


def _mlp_call(om, od, hf, wo, gmlp, wup, wdn, gfin, final):
    tm = TM_MLP
    tokens = hf.shape[0]
    row = lambda w: pl.BlockSpec((tm, w), lambda i: (i, 0))
    return pl.pallas_call(
        functools.partial(_mlp_kernel, final=final),
        grid=(tokens // tm,),
        in_specs=[row(om.shape[1]), row(od.shape[1]), row(D_MODEL),
                  _const_spec(wo.shape), _const_spec(gmlp.shape),
                  _const_spec(wup.shape), _const_spec(wdn.shape), _const_spec(gfin.shape)],
        out_specs=row(D_MODEL),
        out_shape=jax.ShapeDtypeStruct(hf.shape, F32),
        compiler_params=pltpu.CompilerParams(
            dimension_semantics=("arbitrary",), vmem_limit_bytes=VMEM_LIMIT),
        name="outproj_mlp",
    )(om, od, hf, wo, gmlp, wup, wdn, gfin)


def _pack_w_in(w):
    zeros = lambda n: jnp.zeros((D_MODEL, n), w.dtype)
    off_kr = Q_LORA + KV_LORA
    off_dq = off_kr + MLA_ROPE
    kr = jnp.concatenate([zeros(MLA_NOPE), w[:, off_kr:off_dq], zeros(LANES - MLA_NOPE - MLA_ROPE)], 1)
    return jnp.concatenate([w[:, :off_kr], kr, w[:, off_dq:]], axis=1).astype(BF16)


def _pack_w_uq(w):
    hd = MLA_NOPE + MLA_ROPE
    w = w.reshape(Q_LORA, MLA_HEADS, hd)
    w = jnp.pad(w, ((0, 0), (0, 0), (0, LANES - hd)))
    return w.reshape(Q_LORA, MLA_HEADS * LANES).astype(BF16)


def _pack_w_ukv(w):
    w = w.reshape(KV_LORA, MLA_HEADS, MLA_NOPE + MLA_V)
    wk = jnp.pad(w[:, :, :MLA_NOPE], ((0, 0), (0, 0), (0, LANES - MLA_NOPE)))
    wv = w[:, :, MLA_NOPE:]
    pad = jnp.zeros_like(wv)
    wv_even = jnp.concatenate([wv, pad], axis=-1)
    wv_odd = jnp.concatenate([pad, wv], axis=-1)
    odd = (jnp.arange(MLA_HEADS) % 2 == 1)[None, :, None]
    wv = jnp.where(odd, wv_odd, wv_even)
    return jnp.concatenate([wk.reshape(KV_LORA, -1), wv.reshape(KV_LORA, -1)], axis=1).astype(BF16)


def _rope_tables(seq):
    pos = np.arange(seq, dtype=np.float32)
    inv_freq = (1.0 / (ROPE_THETA ** (np.arange(0, MLA_ROPE, 2, dtype=np.float32) / MLA_ROPE))).astype(np.float32)
    ang = jnp.asarray(pos[:, None] * inv_freq[None, :])
    cos, sin = jnp.cos(ang), jnp.sin(ang)
    half = MLA_ROPE // 2
    ones = jnp.ones((seq, MLA_NOPE), F32)
    z = lambda n: jnp.zeros((seq, n), F32)
    tail = LANES - MLA_NOPE - MLA_ROPE
    cos_t = jnp.concatenate([ones, cos, cos, z(tail)], axis=1)
    sa_t = jnp.concatenate([z(MLA_NOPE), -sin, z(half), z(tail)], axis=1)
    sb_t = jnp.concatenate([z(MLA_NOPE), z(half), sin, z(tail)], axis=1)
    return cos_t, sa_t, sb_t


def _v_ones():
    v = np.zeros((1, MLA_HEADS * LANES), np.float32)
    for h in range(MLA_HEADS):
        v[0, h * LANES + (MLA_V if h % 2 == 0 else 0)] = 1.0
    return jnp.asarray(v)


def kernel(x, w_in, g_mix, g_q, g_kv, w_uq, w_ukv, lam_q1, lam_k1, lam_q2, lam_k2,
           g_sub, w_out, g_mlp, w_up, w_down, g_final):
    batch, seq, _ = x.shape
    depth = w_in.shape[0]
    cos_t, sa_t, sb_t = _rope_tables(seq)
    vone = _v_ones()
    hf = x.reshape(batch * seq, D_MODEL)
    for l in range(depth):
        lam_init = 0.8 - 0.6 * math.exp(-0.3 * l)
        qm, km, vm, dq, dk, dv = _proj_call(
            hf, g_mix[l][None], _pack_w_in(w_in[l]), g_q[l][None], g_kv[l][None],
            _pack_w_uq(w_uq[l]), _pack_w_ukv(w_ukv[l]), cos_t, sa_t, sb_t, vone, batch, seq)
        o_mla = _mla_call(qm, km, vm)
        lam_p = jnp.stack([lam_q1[l], lam_k1[l], lam_q2[l], lam_k2[l]]).astype(F32)
        o_diff = _diff_call(lam_p, g_sub[l][None], dq, dk, dv, lam_init)
        hf = _mlp_call(o_mla.reshape(batch * seq, -1), o_diff.reshape(batch * seq, -1), hf,
                       w_out[l].astype(BF16), g_mlp[l][None], w_up[l].astype(BF16),
                       w_down[l].astype(BF16), g_final[None], final=(l == depth - 1))
    return hf.reshape(batch, seq, D_MODEL)
```

```python
import functools
import math

import ml_dtypes
import numpy as np
import jax
import jax.numpy as jnp
from jax import lax
from jax.experimental import pallas as pl
from jax.experimental.pallas import tpu as pltpu

D_MODEL = 1024
MLA_HEADS = 8
MLA_NOPE = 64
MLA_ROPE = 32
MLA_V = 64
Q_LORA = 256
KV_LORA = 256
DIFF_HEADS = 4
DIFF_D = 64
D_FF = 4 * D_MODEL
ROPE_THETA = 10000.0
EPS = 1e-6
SUBLN_EPS = 1e-5

LANES = 128
LOG2E = math.log2(math.e)

C_CQ = 0
C_CKV = C_CQ + Q_LORA
C_KR = C_CKV + KV_LORA
C_DQ = C_KR + LANES
C_DK = C_DQ + DIFF_HEADS * LANES
C_DV = C_DK + DIFF_HEADS * LANES
P_PACK = C_DV + DIFF_HEADS * LANES

TM_PROJ = 1024
TM_MLP = 1024
TQ_MLA = 256
TQ_DIFF = 256
FF_CHUNK = 1024
DIFF_HEADS_PER_STEP = 2
MLA_PAIRS_PER_STEP = 2
KEY_TILE = 256
VMEM_LIMIT = 56 * 1024 * 1024

F32 = jnp.float32
BF16 = jnp.bfloat16


def _rms(x, g, eps):
    return x * lax.rsqrt(jnp.mean(x * x, axis=-1, keepdims=True) + eps) * g


def _dot(a, b):
    return jnp.dot(a, b, preferred_element_type=F32)


def _dot_nt(a, b):
    return lax.dot_general(a, b, (((1,), (1,)), ((), ())), preferred_element_type=F32)


def _proj_kernel(x_ref, gmix_ref, w1_ref, gq_ref, gkv_ref, wq_ref, wkv_ref,
                 cos_ref, sa_ref, sb_ref, vone_ref,
                 qm_ref, km_ref, vm_ref, dq_ref, dk_ref, dv_ref):
    x = x_ref[...]
    u = _rms(x, gmix_ref[...], EPS).astype(BF16)
    z = _dot(u, w1_ref[...])
    cq = _rms(z[:, C_CQ:C_CKV], gq_ref[...], EPS).astype(BF16)
    ckv = _rms(z[:, C_CKV:C_KR], gkv_ref[...], EPS).astype(BF16)
    q = _dot(cq, wq_ref[...])
    kv = _dot(ckv, wkv_ref[...])
    cos = cos_ref[...]
    sa = sa_ref[...]
    sb = sb_ref[...]

    half = MLA_ROPE // 2

    def rope(t):
        return t * cos + pltpu.roll(t, LANES - half, 1) * sa + pltpu.roll(t, half, 1) * sb

    k_rope = rope(z[:, C_KR:C_DQ])
    q_scale = (MLA_NOPE + MLA_ROPE) ** -0.5 * LOG2E
    nk = MLA_HEADS * LANES
    v_all = kv[:, nk:] + vone_ref[...]
    for h in range(MLA_HEADS):
        sl = slice(h * LANES, (h + 1) * LANES)
        qm_ref[0, h] = (rope(q[:, sl]) * q_scale).astype(BF16)
        km_ref[0, h] = (kv[:, sl] + k_rope).astype(BF16)
        vm_ref[0, h] = v_all[:, sl].astype(BF16)
    d_scale = DIFF_D ** -0.5 * LOG2E
    for h in range(DIFF_HEADS):
        dq_ref[0, h] = (z[:, C_DQ + h * LANES:C_DQ + (h + 1) * LANES] * d_scale).astype(BF16)
        dk_ref[0, h] = z[:, C_DK + h * LANES:C_DK + (h + 1) * LANES].astype(BF16)
        dv_ref[0, h] = z[:, C_DV + h * LANES:C_DV + (h + 1) * LANES].astype(BF16)


def _const_spec(shape):
    nd = len(shape)
    return pl.BlockSpec(shape, lambda *_: (0,) * nd, pipeline_mode=pl.Buffered(1))


def _proj_call(xf, gmix, w1, gq, gkv, wq, wkv, cos_t, sa_t, sb_t, vone, batch, seq):
    tm = TM_PROJ
    nsb = seq // tm
    tokens = batch * seq
    head_spec = lambda nh: pl.BlockSpec((1, nh, tm, LANES), lambda i: (i // nsb, 0, i % nsb, 0))
    tab_spec = pl.BlockSpec((tm, LANES), lambda i: (i % nsb, 0))
    mshape = jax.ShapeDtypeStruct((batch, MLA_HEADS, seq, LANES), BF16)
    dshape = jax.ShapeDtypeStruct((batch, DIFF_HEADS, seq, LANES), BF16)
    return pl.pallas_call(
        _proj_kernel,
        grid=(tokens // tm,),
        in_specs=[
            pl.BlockSpec((tm, D_MODEL), lambda i: (i, 0)),
            _const_spec(gmix.shape), _const_spec(w1.shape), _const_spec(gq.shape),
            _const_spec(gkv.shape), _const_spec(wq.shape), _const_spec(wkv.shape),
            tab_spec, tab_spec, tab_spec, _const_spec(vone.shape),
        ],
        out_specs=[head_spec(MLA_HEADS)] * 3 + [head_spec(DIFF_HEADS)] * 3,
        out_shape=[mshape] * 3 + [dshape] * 3,
        compiler_params=pltpu.CompilerParams(
            dimension_semantics=("arbitrary",), vmem_limit_bytes=VMEM_LIMIT),
        name="proj",
    )(xf, gmix, w1, gq, gkv, wq, wkv, cos_t, sa_t, sb_t, vone)


def _mla_kernel(q_ref, k_ref, v_ref, o_ref, s_ref, m_ref, *, seq):
    tq = TQ_MLA
    nb = seq // tq
    nkt = seq // KEY_TILE
    lane = lax.broadcasted_iota(jnp.int32, (tq, LANES), 1)

    def scores(unit, slot):
        pair, blk = divmod(unit, nb)
        for hh in range(2):
            q = q_ref[0, 2 * pair + hh, blk * tq:(blk + 1) * tq, :]
            macc = None
            for kt in range(nkt):
                cols = slice(kt * KEY_TILE, (kt + 1) * KEY_TILE)
                s = _dot_nt(q, k_ref[0, 2 * pair + hh, cols, :])
                s_ref[slot, hh, :, cols] = s
                t = jnp.maximum(s[:, :LANES], s[:, LANES:])
                macc = t if macc is None else jnp.maximum(macc, t)
            m = jnp.max(macc, axis=-1, keepdims=True)
            m_ref[slot, hh] = jnp.broadcast_to(m, (tq, LANES))

    def values(unit, slot):
        pair, blk = divmod(unit, nb)
        acc = None
        for kt in range(nkt):
            cols = slice(kt * KEY_TILE, (kt + 1) * KEY_TILE)
            ps = []
            for hh in range(2):
                mb = m_ref[slot, hh]
                p = jnp.exp2(s_ref[slot, hh, :, cols] - jnp.concatenate([mb, mb], axis=1))
                ps.append(p.astype(BF16))
            vt = jnp.concatenate([v_ref[0, 2 * pair, cols, :], v_ref[0, 2 * pair + 1, cols, :]],
                                 axis=1)
            part = _dot(jnp.concatenate(ps, axis=0), vt)
            acc = part if acc is None else acc + part
        outs = []
        for hh in range(2):
            a = acc[hh * tq:(hh + 1) * tq, hh * LANES:(hh + 1) * LANES]
            lcol = MLA_V if hh == 0 else 0
            outs.append(a * (1.0 / a[:, lcol:lcol + 1]))
        o_ref[0, blk * tq:(blk + 1) * tq, pair * LANES:(pair + 1) * LANES] = (
            jnp.where(lane < MLA_V, outs[0], outs[1]).astype(o_ref.dtype))

    units = MLA_PAIRS_PER_STEP * nb
    scores(0, 0)
    for unit in range(units):
        if unit + 1 < units:
            scores(unit + 1, (unit + 1) % 2)
        values(unit, unit % 2)


def _mla_call(qm, km, vm):
    batch, _, seq, _ = qm.shape
    hps = 2 * MLA_PAIRS_PER_STEP
    in_spec = pl.BlockSpec((1, hps, seq, LANES), lambda b, g: (b, g, 0, 0))
    return pl.pallas_call(
        functools.partial(_mla_kernel, seq=seq),
        grid=(batch, MLA_HEADS // hps),
        in_specs=[in_spec, in_spec, in_spec],
        out_specs=pl.BlockSpec((1, seq, MLA_PAIRS_PER_STEP * LANES), lambda b, g: (b, 0, g)),
        out_shape=jax.ShapeDtypeStruct((batch, seq, MLA_HEADS * MLA_V), BF16),
        scratch_shapes=[pltpu.VMEM((2, 2, TQ_MLA, seq), F32),
                        pltpu.VMEM((2, 2, TQ_MLA, LANES), F32)],
        compiler_params=pltpu.CompilerParams(
            dimension_semantics=("arbitrary", "arbitrary"), vmem_limit_bytes=VMEM_LIMIT),
        name="mla_attn",
    )(qm, km, vm)


def _diff_kernel(lam_ref, gsub_ref, roles_ref, q_ref, k_ref, v_ref, o_ref,
                 bias_ref, kfeat_ref, k12_ref, vaug_ref, s_ref, m_ref, *, seq, lam_init):
    tq = TQ_DIFF
    g_idx = pl.program_id(0)
    b = pl.program_id(1)
    nb = seq // tq
    nkt = seq // KEY_TILE
    assert tq == KEY_TILE
    hps = DIFF_HEADS_PER_STEP

    slopes = []
    for hd in range(hps):
        hv = jnp.full((1, 1), g_idx * hps + hd, jnp.int32)
        slope = jnp.zeros((1, 1), F32)
        for hh in range(DIFF_HEADS):
            slope = jnp.where(hv == hh, 2.0 ** (-8.0 * (hh + 1.0) / DIFF_HEADS), slope)
        slopes.append(slope)

    @pl.when(b == 0)
    def _():
        r = lax.broadcasted_iota(jnp.int32, (tq, KEY_TILE), 0)
        c = lax.broadcasted_iota(jnp.int32, (tq, KEY_TILE), 1)
        dist = jnp.abs(r - c).astype(F32)
        j = lax.broadcasted_iota(jnp.int32, (seq, LANES), 0)
        j_tile = (j - (j & (KEY_TILE - 1))).astype(F32)
        j_in = (j & (KEY_TILE - 1)).astype(F32)
        for hd in range(hps):
            bias_ref[hd] = dist * (-LOG2E * slopes[hd])
            for mi in range(2):
                kf = (slopes[hd] * roles_ref[mi, 0:1] + roles_ref[mi, 1:2] * j_tile
                      + roles_ref[mi, 2:3] * j_in)
                kfeat_ref[hd, mi] = kf.astype(BF16)

    lp = lam_ref[...]
    lam = (jnp.exp(jnp.sum(lp[0:1] * lp[1:2], axis=-1, keepdims=True))
           - jnp.exp(jnp.sum(lp[2:3] * lp[3:4], axis=-1, keepdims=True)) + lam_init)

    lane_k = lax.broadcasted_iota(jnp.int32, (seq, LANES), 1)
    for hd in range(hps):
        k = k_ref[0, hd]
        k12_ref[hd, 0] = jnp.where(lane_k < DIFF_D, k, kfeat_ref[hd, 0])
        k12_ref[hd, 1] = jnp.where(lane_k >= DIFF_D, k, kfeat_ref[hd, 1])
        vaug_ref[hd, :, :LANES] = v_ref[0, hd]
        vaug_ref[hd, :, LANES:] = (lane_k == 0).astype(BF16)
    g = gsub_ref[...] * (1.0 - lam_init)
    lane_q = lax.broadcasted_iota(jnp.int32, (tq, LANES), 1)
    i_in = lax.broadcasted_iota(jnp.int32, (tq, LANES), 0).astype(F32)

    def scores(unit, slot):
        hd, blk = divmod(unit, nb)
        q = q_ref[0, hd, blk * tq:(blk + 1) * tq, :]
        for mi in range(2):
            data = (lane_q < DIFF_D) if mi == 0 else (lane_q >= DIFF_D)
            qf = (roles_ref[mi, 3:4] * float(blk * tq) + roles_ref[mi, 4:5] * i_in
                  + slopes[hd] * roles_ref[mi, 5:6])
            lhs = {0: jnp.where(data, q, jnp.zeros_like(q))}
            if blk > 0:
                lhs[1] = jnp.where(data, q, qf.astype(BF16))
            if blk < nb - 1:
                lhs[-1] = jnp.where(data, q, (-qf).astype(BF16))
            macc = None
            for kt in range(nkt):
                cols = slice(kt * KEY_TILE, (kt + 1) * KEY_TILE)
                s = _dot_nt(lhs[(blk > kt) - (blk < kt)], k12_ref[hd, mi, cols, :])
                if kt == blk:
                    s = s + bias_ref[hd]
                s_ref[slot, mi, :, cols] = s
                t = jnp.maximum(s[:, :LANES], s[:, LANES:])
                macc = t if macc is None else jnp.maximum(macc, t)
            m = jnp.max(macc, axis=-1, keepdims=True)
            m_ref[slot, mi] = jnp.broadcast_to(m, (tq, LANES))

    def values(unit, slot):
        hd, blk = divmod(unit, nb)
        outs = []
        for mi in range(2):
            acc = None
            for kt in range(nkt):
                cols = slice(kt * KEY_TILE, (kt + 1) * KEY_TILE)
                mb = m_ref[slot, mi]
                p = jnp.exp2(s_ref[slot, mi, :, cols] - jnp.concatenate([mb, mb], axis=1))
                part = _dot(p.astype(BF16), vaug_ref[hd, cols, :])
                acc = part if acc is None else acc + part
            outs.append(acc[:, :LANES] * (1.0 / acc[:, LANES:LANES + 1]))
        o = outs[0] - lam * outs[1]
        o_ref[0, blk * tq:(blk + 1) * tq, hd * LANES:(hd + 1) * LANES] = (
            _rms(o, g, SUBLN_EPS).astype(o_ref.dtype))

    units = hps * nb
    scores(0, 0)
    for unit in range(units):
        if unit + 1 < units:
            scores(unit + 1, (unit + 1) % 2)
        values(unit, unit % 2)


def _alibi_roles():
    terms = []
    rest = LOG2E
    for _ in range(3):
        t = float(np.float32(rest).astype(ml_dtypes.bfloat16))
        terms.append(t)
        rest -= t
    roles = np.zeros((2, 6, LANES), np.float32)
    for mi in range(2):
        base = DIFF_D if mi == 0 else 0
        for t, lt in enumerate(terms):
            roles[mi, 0, base + t] = -lt
            roles[mi, 0, base + 3 + t] = -lt
            roles[mi, 1, base + 6 + t] = 1.0
            roles[mi, 2, base + 9 + t] = 1.0
            roles[mi, 3, base + t] = 1.0
            roles[mi, 4, base + 3 + t] = 1.0
            roles[mi, 5, base + 6 + t] = lt
            roles[mi, 5, base + 9 + t] = lt
    return jnp.asarray(roles)


def _diff_call(lam_p, gsub, dq, dk, dv, lam_init):
    batch, _, seq, _ = dq.shape
    roles = _alibi_roles()
    hps = DIFF_HEADS_PER_STEP
    in_spec = pl.BlockSpec((1, hps, seq, LANES), lambda g, b: (b, g, 0, 0))
    return pl.pallas_call(
        functools.partial(_diff_kernel, seq=seq, lam_init=lam_init),
        grid=(DIFF_HEADS // hps, batch),
        in_specs=[_const_spec(lam_p.shape), _const_spec(gsub.shape), _const_spec(roles.shape),
                  in_spec, in_spec, in_spec],
        out_specs=pl.BlockSpec((1, seq, hps * LANES), lambda g, b: (b, 0, g)),
        out_shape=jax.ShapeDtypeStruct((batch, seq, DIFF_HEADS * LANES), BF16),
        scratch_shapes=[pltpu.VMEM((hps, TQ_DIFF, KEY_TILE), F32),
                        pltpu.VMEM((hps, 2, seq, LANES), BF16),
                        pltpu.VMEM((hps, 2, seq, LANES), BF16),
                        pltpu.VMEM((hps, seq, 2 * LANES), BF16),
                        pltpu.VMEM((2, 2, TQ_DIFF, seq), F32),
                        pltpu.VMEM((2, 2, TQ_DIFF, LANES), F32)],
        compiler_params=pltpu.CompilerParams(
            dimension_semantics=("arbitrary", "arbitrary"), vmem_limit_bytes=VMEM_LIMIT),
        name="diff_attn",
    )(lam_p, gsub, roles, dq, dk, dv)


def _mlp_kernel(om_ref, od_ref, h_ref, wo1_ref, wo2_ref, gmlp_ref, wup_ref, wdn_ref, gfin_ref,
                out_ref, *, final):
    a = h_ref[...] + _dot(om_ref[...], wo1_ref[...]) + _dot(od_ref[...], wo2_ref[...])
    u = _rms(a, gmlp_ref[...], EPS).astype(BF16)
    out_ref[...] = a
    for c in range(D_FF // FF_CHUNK):
        f = jnp.maximum(_dot(u, wup_ref[:, c * FF_CHUNK:(c + 1) * FF_CHUNK]), 0.0)
        out_ref[...] += _dot((f * f).astype(BF16), wdn_ref[c * FF_CHUNK:(c + 1) * FF_CHUNK, :])
    if final:
        out_ref[...] = _rms(out_ref[...], gfin_ref[...], EPS)


def _mlp_call(om, od, hf, wo1, wo2, gmlp, wup, wdn, gfin, final):
    tm = TM_MLP
    tokens = hf.shape[0]
    row = lambda w: pl.BlockSpec((tm, w), lambda i: (i, 0))
    return pl.pallas_call(
        functools.partial(_mlp_kernel, final=final),
        grid=(tokens // tm,),
        in_specs=[row(om.shape[1]), row(od.shape[1]), row(D_MODEL),
                  _const_spec(wo1.shape), _const_spec(wo2.shape), _const_spec(gmlp.shape),
                  _const_spec(wup.shape), _const_spec(wdn.shape), _const_spec(gfin.shape)],
        out_specs=row(D_MODEL),
        out_shape=jax.ShapeDtypeStruct(hf.shape, F32),
        compiler_params=pltpu.CompilerParams(
            dimension_semantics=("arbitrary",), vmem_limit_bytes=VMEM_LIMIT),
        name="outproj_mlp",
    )(om, od, hf, wo1, wo2, gmlp, wup, wdn, gfin)


def _pack_w_in(w):
    zeros = lambda n: jnp.zeros((D_MODEL, n), w.dtype)
    off_kr = Q_LORA + KV_LORA
    off_dq = off_kr + MLA_ROPE
    kr = jnp.concatenate([zeros(MLA_NOPE), w[:, off_kr:off_dq], zeros(LANES - MLA_NOPE - MLA_ROPE)], 1)
    return jnp.concatenate([w[:, :off_kr], kr, w[:, off_dq:]], axis=1).astype(BF16)


def _pack_w_uq(w):
    hd = MLA_NOPE + MLA_ROPE
    w = w.reshape(Q_LORA, MLA_HEADS, hd)
    w = jnp.pad(w, ((0, 0), (0, 0), (0, LANES - hd)))
    return w.reshape(Q_LORA, MLA_HEADS * LANES).astype(BF16)


def _pack_w_ukv(w):
    w = w.reshape(KV_LORA, MLA_HEADS, MLA_NOPE + MLA_V)
    wk = jnp.pad(w[:, :, :MLA_NOPE], ((0, 0), (0, 0), (0, LANES - MLA_NOPE)))
    wv = w[:, :, MLA_NOPE:]
    pad = jnp.zeros_like(wv)
    wv_even = jnp.concatenate([wv, pad], axis=-1)
    wv_odd = jnp.concatenate([pad, wv], axis=-1)
    odd = (jnp.arange(MLA_HEADS) % 2 == 1)[None, :, None]
    wv = jnp.where(odd, wv_odd, wv_even)
    return jnp.concatenate([wk.reshape(KV_LORA, -1), wv.reshape(KV_LORA, -1)], axis=1).astype(BF16)


def _rope_tables(seq):
    pos = np.arange(seq, dtype=np.float32)
    inv_freq = (1.0 / (ROPE_THETA ** (np.arange(0, MLA_ROPE, 2, dtype=np.float32) / MLA_ROPE))).astype(np.float32)
    ang = jnp.asarray(pos[:, None] * inv_freq[None, :])
    cos, sin = jnp.cos(ang), jnp.sin(ang)
    half = MLA_ROPE // 2
    ones = jnp.ones((seq, MLA_NOPE), F32)
    z = lambda n: jnp.zeros((seq, n), F32)
    tail = LANES - MLA_NOPE - MLA_ROPE
    cos_t = jnp.concatenate([ones, cos, cos, z(tail)], axis=1)
    sa_t = jnp.concatenate([z(MLA_NOPE), -sin, z(half), z(tail)], axis=1)
    sb_t = jnp.concatenate([z(MLA_NOPE), z(half), sin, z(tail)], axis=1)
    return cos_t, sa_t, sb_t


def _v_ones():
    v = np.zeros((1, MLA_HEADS * LANES), np.float32)
    for h in range(MLA_HEADS):
        v[0, h * LANES + (MLA_V if h % 2 == 0 else 0)] = 1.0
    return jnp.asarray(v)


def kernel(x, w_in, g_mix, g_q, g_kv, w_uq, w_ukv, lam_q1, lam_k1, lam_q2, lam_k2,
           g_sub, w_out, g_mlp, w_up, w_down, g_final):
    batch, seq, _ = x.shape
    depth = w_in.shape[0]
    cos_t, sa_t, sb_t = _rope_tables(seq)
    vone = _v_ones()
    hf = x.reshape(batch * seq, D_MODEL)
    for l in range(depth):
        lam_init = 0.8 - 0.6 * math.exp(-0.3 * l)
        qm, km, vm, dq, dk, dv = _proj_call(
            hf, g_mix[l][None], _pack_w_in(w_in[l]), g_q[l][None], g_kv[l][None],
            _pack_w_uq(w_uq[l]), _pack_w_ukv(w_ukv[l]), cos_t, sa_t, sb_t, vone, batch, seq)
        o_mla = _mla_call(qm, km, vm)
        lam_p = jnp.stack([lam_q1[l], lam_k1[l], lam_q2[l], lam_k2[l]]).astype(F32)
        o_diff = _diff_call(lam_p, g_sub[l][None], dq, dk, dv, lam_init)
        wo = w_out[l].astype(BF16)
        n_mla = MLA_HEADS * MLA_V
        hf = _mlp_call(o_mla.reshape(batch * seq, -1), o_diff.reshape(batch * seq, -1), hf,
                       wo[:n_mla], wo[n_mla:], g_mlp[l][None], w_up[l].astype(BF16),
                       w_down[l].astype(BF16), g_final[None], final=(l == depth - 1))
    return hf.reshape(batch, seq, D_MODEL)
```
